```python
import jax, jax.numpy as jnp
from jax import lax
import numpy as np

D_MODEL = 1024
BATCH = 2
SEQ = 8192
DEPTH = 1

DN_HEADS = 8
DN_HEAD_DIM = 128
DN_WIDTH = DN_HEADS * DN_HEAD_DIM
CONV_K = 4
CHUNK = 64
S5_WIDTH = D_MODEL // 2
S5_GROUP = 16
S5_GROUPS = S5_WIDTH // S5_GROUP
S5_STATE = 64
EPS = 1e-6
IN_COLS = 4 * DN_WIDTH + 2 * DN_HEADS + 2 * S5_WIDTH + 2 * D_MODEL

kernel_name = 'hybrid_deltanet_s5_adaln_block'


def rmsnorm(x):
    xf = x.astype(jnp.float32)
    return (xf * lax.rsqrt(jnp.mean(xf * xf, axis=-1, keepdims=True) + EPS)).astype(x.dtype)


def l2norm(x):
    return x * lax.rsqrt(jnp.sum(x * x, axis=-1, keepdims=True) + EPS)


def causal_depthwise_conv(x, w):
    k = w.shape[0]
    return lax.conv_general_dilated(
        x, w[:, None, :], window_strides=(1,), padding=[(k - 1, 0)],
        dimension_numbers=('NWC', 'WIO', 'NWC'), feature_group_count=x.shape[-1])


def gated_delta_rule(q, k, v, g, beta):
    b, l, h, dk = q.shape
    dv = v.shape[-1]
    n = l // CHUNK

    def chunks(t):
        t = t.reshape((b, n, CHUNK, h) + t.shape[3:])
        return jnp.moveaxis(t, (1, 3), (0, 2))

    qc, kc, vc, bc = chunks(q), chunks(k), chunks(v), chunks(beta)
    gc = jnp.cumsum(chunks(g), axis=-1)
    causal = jnp.tril(jnp.ones((CHUNK, CHUNK), bool))
    strict = jnp.tril(jnp.ones((CHUNK, CHUNK), bool), -1)
    decay = jnp.exp(jnp.where(causal, gc[..., :, None] - gc[..., None, :], -jnp.inf))
    kk = jnp.einsum('nbhid,nbhjd->nbhij', kc, kc)
    tmat = jnp.eye(CHUNK, dtype=q.dtype) + jnp.where(strict, bc[..., :, None] * kk * decay, 0.0)
    u = lax.linalg.triangular_solve(tmat, vc * bc[..., None], left_side=True, lower=True, unit_diagonal=True)
    w = lax.linalg.triangular_solve(tmat, kc * (bc * jnp.exp(gc))[..., None], left_side=True, lower=True, unit_diagonal=True)
    attn = jnp.where(causal, jnp.einsum('nbhid,nbhjd->nbhij', qc, kc) * decay, 0.0)
    q_dec = qc * jnp.exp(gc)[..., None]
    k_dec = kc * jnp.exp(gc[..., -1:] - gc)[..., None]
    g_last = jnp.exp(gc[..., -1])

    def step(state, inp):
        u_i, w_i, attn_i, qd_i, kd_i, gl_i = inp
        v_new = u_i - jnp.einsum('bhcd,bhde->bhce', w_i, state)
        o = jnp.einsum('bhcd,bhde->bhce', qd_i, state) + jnp.einsum('bhij,bhje->bhie', attn_i, v_new)
        state = state * gl_i[..., None, None] + jnp.einsum('bhcd,bhce->bhde', kd_i, v_new)
        return state, o

    s0 = jnp.zeros((b, h, dk, dv), q.dtype)
    _, o = lax.scan(step, s0, (u, w, attn, q_dec, k_dec, g_last))
    return jnp.moveaxis(o, (0, 2), (1, 3)).reshape(b, l, h, dv)


def deltanet_branch(qkv, z, beta_in, a_in, conv_w, a_log, dt_bias, norm_w, w_proj):
    b, l, _ = qkv.shape
    f32 = jnp.float32
    qkv = jax.nn.silu(causal_depthwise_conv(qkv, conv_w)).astype(f32)
    q, k, v = jnp.split(qkv, 3, axis=-1)
    q = l2norm(q.reshape(b, l, DN_HEADS, DN_HEAD_DIM)) * (DN_HEAD_DIM ** -0.5)
    k = l2norm(k.reshape(b, l, DN_HEADS, DN_HEAD_DIM))
    v = v.reshape(b, l, DN_HEADS, DN_HEAD_DIM)
    beta = jax.nn.sigmoid(beta_in.astype(f32))
    g = -jnp.exp(a_log.astype(f32)) * jax.nn.softplus(a_in.astype(f32) + dt_bias.astype(f32))
    o = gated_delta_rule(q, k, v, g, beta)
    o = (rmsnorm(o) * norm_w.astype(f32)).reshape(b, l, DN_WIDTH).astype(z.dtype)
    return (o * jax.nn.silu(z)) @ w_proj


def s5_scan(u, lam_re, lam_im, log_dt, b_re, b_im, c_re, c_im, d_skip):
    bsz, l, _ = u.shape
    ug = u.reshape(bsz, l, S5_GROUPS, S5_GROUP)
    dt = jnp.exp(log_dt)[:, None]
    lr = jnp.minimum(lam_re, -1e-4)
    li = lam_im
    mag = jnp.exp(lr * dt)
    ab_re = mag * jnp.cos(li * dt)
    ab_im = mag * jnp.sin(li * dt)
    den = lr * lr + li * li
    f_re = ((ab_re - 1.0) * lr + ab_im * li) / den
    f_im = (ab_im * lr - (ab_re - 1.0) * li) / den
    bb_re = f_re[..., None] * b_re - f_im[..., None] * b_im
    bb_im = f_re[..., None] * b_im + f_im[..., None] * b_re
    bu_re = jnp.einsum('blgi,gpi->blgp', ug, bb_re)
    bu_im = jnp.einsum('blgi,gpi->blgp', ug, bb_im)
    shape_a = (1, l, S5_GROUPS, S5_STATE)
    a_re = jnp.broadcast_to(ab_re[None, None], shape_a)
    a_im = jnp.broadcast_to(ab_im[None, None], shape_a)

    def combine(e1, e2):
        a1r, a1i, b1r, b1i = e1
        a2r, a2i, b2r, b2i = e2
        return (a1r * a2r - a1i * a2i, a1r * a2i + a1i * a2r,
                a2r * b1r - a2i * b1i + b2r, a2r * b1i + a2i * b1r + b2i)

    _, _, xr, xi = lax.associative_scan(combine, (a_re, a_im, bu_re, bu_im), axis=1)
    y = jnp.einsum('blgp,gip->blgi', xr, c_re) - jnp.einsum('blgp,gip->blgi', xi, c_im)
    return y.reshape(bsz, l, S5_WIDTH) + d_skip * u


def s5_branch(u, z, lam_re, lam_im, log_dt, b_re, b_im, c_re, c_im, d_skip, glu_w, glu_b, w_proj):
    f32 = jnp.float32
    y = s5_scan(u.astype(f32), lam_re.astype(f32), lam_im.astype(f32), log_dt.astype(f32),
                b_re.astype(f32), b_im.astype(f32), c_re.astype(f32), c_im.astype(f32), d_skip.astype(f32))
    gy = jax.nn.gelu(y)
    y = gy * jax.nn.sigmoid(gy @ glu_w.astype(f32) + glu_b.astype(f32))
    return (y.astype(z.dtype) * jax.nn.silu(z)) @ w_proj


def setup_inputs(seed: int = 0) -> dict:
    key = jax.random.key(seed)
    ks = jax.random.split(key, 24)
    nrm = jax.random.normal
    L_, D_ = DEPTH, D_MODEL
    dt_dn = jnp.exp(jax.random.uniform(ks[4], (L_, DN_HEADS), minval=np.log(1e-3), maxval=np.log(1e-1)))
    lam_re = -0.5 + 0.01 * nrm(ks[8], (L_, S5_GROUPS, S5_STATE))
    lam_im = jnp.broadcast_to(np.pi * jnp.arange(S5_STATE, dtype=jnp.float32), (L_, S5_GROUPS, S5_STATE))
    return {
        'x': nrm(ks[0], (BATCH, SEQ, D_)),
        'c': nrm(ks[1], (BATCH, D_)),
        'w_ada': nrm(ks[2], (L_, D_, 3 * D_)) * (0.5 * D_ ** -0.5),
        'b_ada': 0.01 * nrm(ks[3], (L_, 3 * D_)),
        'w_in': nrm(ks[5], (L_, D_, IN_COLS)) * D_ ** -0.5,
        'dn_conv_w': nrm(ks[6], (L_, CONV_K, 3 * DN_WIDTH)) * CONV_K ** -0.5,
        'dn_a_log': jnp.log(jax.random.uniform(ks[7], (L_, DN_HEADS), minval=1.0, maxval=16.0)),
        'dn_dt_bias': dt_dn + jnp.log(-jnp.expm1(-dt_dn)),
        'dn_norm_w': 1.0 + 0.02 * nrm(ks[9], (L_, DN_HEAD_DIM)),
        'w_proj_a': nrm(ks[10], (L_, DN_WIDTH, D_)) * DN_WIDTH ** -0.5,
        's5_lambda_re': lam_re,
        's5_lambda_im': lam_im,
        's5_log_dt': jax.random.uniform(ks[11], (L_, S5_GROUPS), minval=np.log(1e-3), maxval=np.log(1e-1)),
        's5_b_re': nrm(ks[12], (L_, S5_GROUPS, S5_STATE, S5_GROUP)) * (2 * S5_GROUP) ** -0.5,
        's5_b_im': nrm(ks[13], (L_, S5_GROUPS, S5_STATE, S5_GROUP)) * (2 * S5_GROUP) ** -0.5,
        's5_c_re': nrm(ks[14], (L_, S5_GROUPS, S5_GROUP, S5_STATE)) * S5_STATE ** -0.5,
        's5_c_im': nrm(ks[15], (L_, S5_GROUPS, S5_GROUP, S5_STATE)) * S5_STATE ** -0.5,
        's5_d': nrm(ks[16], (L_, S5_WIDTH)),
        's5_glu_w': nrm(ks[17], (L_, S5_WIDTH, S5_WIDTH)) * S5_WIDTH ** -0.5,
        's5_glu_b': 0.01 * nrm(ks[18], (L_, S5_WIDTH)),
        'w_proj_b': nrm(ks[19], (L_, S5_WIDTH, D_)) * S5_WIDTH ** -0.5,
        'w_out': nrm(ks[20], (L_, D_, D_)) * D_ ** -0.5,
        'final_norm_w': 1.0 + 0.02 * nrm(ks[21], (D_,)),
    }


def reference(x, c, w_ada, b_ada, w_in, dn_conv_w, dn_a_log, dn_dt_bias, dn_norm_w, w_proj_a,
              s5_lambda_re, s5_lambda_im, s5_log_dt, s5_b_re, s5_b_im, s5_c_re, s5_c_im, s5_d,
              s5_glu_w, s5_glu_b, w_proj_b, w_out, final_norm_w):
    sizes = [3 * DN_WIDTH, DN_WIDTH, DN_HEADS, DN_HEADS, S5_WIDTH, S5_WIDTH, D_MODEL, D_MODEL]
    split_idx = np.cumsum(sizes)[:-1].tolist()
    for layer in range(DEPTH):
        mod = c @ w_ada[layer] + b_ada[layer]
        shift, scale, gate = jnp.split(mod, 3, axis=-1)
        h = rmsnorm(x) * (1.0 + scale[:, None, :]) + shift[:, None, :]
        proj = h @ w_in[layer]
        qkv, z_a, beta_in, a_in, u_s5, z_b, gate_a, gate_b = jnp.split(proj, split_idx, axis=-1)
        y_a = deltanet_branch(qkv, z_a, beta_in, a_in, dn_conv_w[layer], dn_a_log[layer],
                              dn_dt_bias[layer], dn_norm_w[layer], w_proj_a[layer])
        y_b = s5_branch(u_s5, z_b, s5_lambda_re[layer], s5_lambda_im[layer], s5_log_dt[layer],
                        s5_b_re[layer], s5_b_im[layer], s5_c_re[layer], s5_c_im[layer], s5_d[layer],
                        s5_glu_w[layer], s5_glu_b[layer], w_proj_b[layer])
        merged = jax.nn.sigmoid(gate_a) * y_a + jax.nn.sigmoid(gate_b) * y_b
        x = x + gate[:, None, :] * (merged @ w_out[layer])
    return rmsnorm(x) * final_norm_w
```

```python
import functools
import math

import jax
import jax.numpy as jnp
import numpy as np
from jax import lax
from jax.experimental import pallas as pl
from jax.experimental.pallas import tpu as pltpu

D_MODEL = 1024
DN_HEADS = 8
DN_HEAD_DIM = 128
DN_WIDTH = DN_HEADS * DN_HEAD_DIM
CONV_K = 4
CHUNK = 64
S5_WIDTH = D_MODEL // 2
S5_GROUP = 16
S5_GROUPS = S5_WIDTH // S5_GROUP
S5_STATE = 64
S5_T = 16
S5_TC = S5_T * S5_GROUP
EPS = 1e-6
LANES = 128
VMEM_LIMIT = 56 * 1024 * 1024

BF16 = jnp.bfloat16
F32 = jnp.float32


def _cparams(sem):
    return pltpu.CompilerParams(dimension_semantics=sem, vmem_limit_bytes=VMEM_LIMIT)


def _sigmoid(x):
    return 1.0 / (1.0 + jnp.exp(-x))


def _silu(x):
    return x * _sigmoid(x)


def _dot(a, b):
    return jnp.dot(a.astype(BF16), b.astype(BF16), preferred_element_type=F32)


def _dot_nt(a, b):
    return lax.dot_general(a.astype(BF16), b.astype(BF16), (((1,), (1,)), ((), ())),
                           preferred_element_type=F32)


def _dot_tn(a, b):
    return lax.dot_general(a.astype(BF16), b.astype(BF16), (((0,), (0,)), ((), ())),
                           preferred_element_type=F32)


def _ada_kernel(c_ref, w_ref, b_ref, o_ref):
    o_ref[...] = jnp.dot(c_ref[...], w_ref[...], preferred_element_type=F32,
                         precision=lax.Precision.HIGHEST) + b_ref[...]


def _ada(c8, w_ada, b_ada):
    n3 = w_ada.shape[1]
    tn = 1024
    return pl.pallas_call(
        _ada_kernel,
        grid=(n3 // tn,),
        in_specs=[pl.BlockSpec((8, D_MODEL), lambda j: (0, 0)),
                  pl.BlockSpec((D_MODEL, tn), lambda j: (0, j)),
                  pl.BlockSpec((1, tn), lambda j: (0, j))],
        out_specs=pl.BlockSpec((8, tn), lambda j: (0, j)),
        out_shape=jax.ShapeDtypeStruct((8, n3), F32),
        compiler_params=_cparams(("arbitrary",)),
        name="ada",
    )(c8, w_ada, b_ada)


INPROJ_TM = 256
INPROJ_TN = 512


def _inproj_kernel(x_ref, mod_ref, w1_ref, wba_ref, w2_ref,
                   qkv_ref, za_ref, ba_ref, u_ref, zb_ref, ga_ref, gb_ref):
    x = x_ref[...]
    xn = x * lax.rsqrt(jnp.mean(x * x, axis=-1, keepdims=True) + EPS)
    shift = mod_ref[0, :, 0:D_MODEL]
    scale = mod_ref[0, :, D_MODEL:2 * D_MODEL]
    h = (xn * (1.0 + scale) + shift).astype(BF16)

    def cols(w_ref, lo, out_ref, width):
        for j in range(width // INPROJ_TN):
            out_ref[:, j * INPROJ_TN:(j + 1) * INPROJ_TN] = jnp.dot(
                h, w_ref[:, lo + j * INPROJ_TN: lo + (j + 1) * INPROJ_TN],
                preferred_element_type=F32)

    cols(w1_ref, 0, qkv_ref, 3 * DN_WIDTH)
    cols(w1_ref, 3 * DN_WIDTH, za_ref, DN_WIDTH)
    ba_ref[...] = jnp.dot(h, wba_ref[...], preferred_element_type=F32)
    cols(w2_ref, 0, u_ref, S5_WIDTH)
    cols(w2_ref, S5_WIDTH, zb_ref, S5_WIDTH)
    cols(w2_ref, 2 * S5_WIDTH, ga_ref, D_MODEL)
    cols(w2_ref, 2 * S5_WIDTH + D_MODEL, gb_ref, D_MODEL)


def _inproj(x2, mod3, w1, wba, w2, seq):
    n = x2.shape[0]
    tm = INPROJ_TM
    per_b = seq // tm
    row = lambda i: (i, 0)
    const = lambda i: (0, 0)
    widths = [3 * DN_WIDTH, DN_WIDTH, LANES, S5_WIDTH, S5_WIDTH, D_MODEL, D_MODEL]
    return pl.pallas_call(
        _inproj_kernel,
        grid=(n // tm,),
        in_specs=[pl.BlockSpec((tm, D_MODEL), row),
                  pl.BlockSpec((1, 1, 3 * D_MODEL), lambda i: (i // per_b, 0, 0)),
                  pl.BlockSpec(w1.shape, const),
                  pl.BlockSpec(wba.shape, const),
                  pl.BlockSpec(w2.shape, const)],
        out_specs=[pl.BlockSpec((tm, w), row) for w in widths],
        out_shape=[jax.ShapeDtypeStruct((n, w), F32) for w in widths],
        compiler_params=_cparams(("arbitrary",)),
        name="inproj",
    )(x2, mod3, w1, wba, w2)


DELTA_LBLK = 256
PAD = 8


INV_BLOCK = 16


def _unit_lower_inverse(amat, eye, rblk, cblk):
    c = amat.shape[0]
    nb = c // INV_BLOCK
    acols = jnp.zeros((c, INV_BLOCK), F32)
    for b in range(nb):
        acols = acols + jnp.where(rblk[:, 0:INV_BLOCK] == b, amat[:, b * INV_BLOCK:(b + 1) * INV_BLOCK], 0.0)
    t = eye
    for j in range(INV_BLOCK - 1):
        trow = jnp.broadcast_to(t.reshape(nb, INV_BLOCK, c)[:, j:j + 1, :], (nb, INV_BLOCK, c)).reshape(c, c)
        t = t - acols[:, j:j + 1] * trow
    size = INV_BLOCK
    while size < c:
        same_outer = (rblk // (2 * size // INV_BLOCK)) == (cblk // (2 * size // INV_BLOCK))
        same_inner = (rblk // (size // INV_BLOCK)) == (cblk // (size // INV_BLOCK))
        off = jnp.where(same_outer & jnp.logical_not(same_inner), amat, 0.0)
        t = t - _dot(t, _dot(off, t))
        size *= 2
    return t


def _delta_kernel(q_ref, k_ref, v_ref, za_ref, ba_ref, cwq_ref, cwk_ref, cwv_ref,
                  gpar_ref, nw_ref, o_ref,
                  pq_s, pk_s, pv_s, qn_s, kn_s, vv_s, st_s, gc_s, beta_s, gct_s):
    lblk = DELTA_LBLK
    nchunk = lblk // CHUNK
    first = pl.program_id(1) == 0

    @pl.when(first)
    def _():
        zero = jnp.zeros((PAD, DN_WIDTH), F32)
        pq_s[0:PAD, :] = zero
        pk_s[0:PAD, :] = zero
        pv_s[0:PAD, :] = zero
        st_s[...] = jnp.zeros_like(st_s)

    def conv(src_ref, pad_s, cw_ref, dst_s, l2, post_scale):
        pad_s[PAD:PAD + lblk, :] = src_ref[0]
        acc = pad_s[PAD:PAD + lblk, :] * cw_ref[CONV_K - 1:CONV_K, :]
        for j in range(CONV_K - 1):
            sh = CONV_K - 1 - j
            acc = acc + pad_s[PAD - sh:PAD - sh + lblk, :] * cw_ref[j:j + 1, :]
        y = _silu(acc)
        pad_s[0:PAD, :] = pad_s[lblk:lblk + PAD, :]
        if l2:
            for h in range(DN_HEADS):
                yh = y[:, h * DN_HEAD_DIM:(h + 1) * DN_HEAD_DIM]
                inv = lax.rsqrt(jnp.sum(yh * yh, axis=-1, keepdims=True) + EPS)
                dst_s[:, h * DN_HEAD_DIM:(h + 1) * DN_HEAD_DIM] = yh * (inv * post_scale)
        else:
            dst_s[...] = y

    conv(q_ref, pq_s, cwq_ref, qn_s, True, DN_HEAD_DIM ** -0.5)
    conv(k_ref, pk_s, cwk_ref, kn_s, True, 1.0)
    conv(v_ref, pv_s, cwv_ref, vv_s, False, 1.0)

    ba = ba_ref[0]
    beta_s[...] = _sigmoid(ba)
    zg = ba + gpar_ref[1:2, :]
    softplus = jnp.maximum(zg, 0.0) + jnp.log(1.0 + jnp.exp(-jnp.abs(zg)))
    g = -jnp.exp(gpar_ref[0:1, :]) * softplus
    rowi = lax.broadcasted_iota(jnp.int32, (lblk, LANES), 0) & (CHUNK - 1)
    d = 1
    while d < CHUNK:
        g = g + jnp.where(rowi >= d, pltpu.roll(g, d, axis=0), 0.0)
        d *= 2
    gc_s[...] = g
    for c in range(nchunk):
        blk = jnp.concatenate([g[c * CHUNK:(c + 1) * CHUNK, :], jnp.zeros((LANES - CHUNK, LANES), F32)], axis=0)
        gct_s[c] = blk.T

    ri = lax.broadcasted_iota(jnp.int32, (CHUNK, CHUNK), 0)
    ci = lax.broadcasted_iota(jnp.int32, (CHUNK, CHUNK), 1)
    causal = ri >= ci
    strict = ri > ci
    eye = (ri == ci).astype(F32)
    rblk = ri // INV_BLOCK
    cblk = ci // INV_BLOCK

    def chunk_body(c, carry):
        r0 = pl.multiple_of(c * CHUNK, CHUNK)
        gcc = gc_s[pl.ds(r0, CHUNK), :]
        betac = beta_s[pl.ds(r0, CHUNK), :]
        gct = gct_s[c]
        for h in range(DN_HEADS):
            lo, hi = h * DN_HEAD_DIM, (h + 1) * DN_HEAD_DIM
            gcol = gcc[:, DN_HEADS + h:DN_HEADS + h + 1]
            grow = gct[DN_HEADS + h:DN_HEADS + h + 1, 0:CHUNK]
            glast = gcc[CHUNK - 1:CHUNK, DN_HEADS + h:DN_HEADS + h + 1]
            bcol = betac[:, h:h + 1]
            dec = jnp.where(causal, jnp.exp(jnp.minimum(gcol - grow, 0.0)), 0.0)
            qh = qn_s[pl.ds(r0, CHUNK), lo:hi]
            kh = kn_s[pl.ds(r0, CHUNK), lo:hi]
            vh = vv_s[pl.ds(r0, CHUNK), lo:hi]
            qk = _dot_nt(jnp.concatenate([qh, kh], axis=0), kh)
            attn = qk[0:CHUNK] * dec
            amat = jnp.where(strict, bcol * qk[CHUNK:2 * CHUNK] * dec, 0.0)
            pmat = _unit_lower_inverse(amat, eye, rblk, cblk)
            eg = jnp.exp(gcol)
            rhs = jnp.concatenate([vh * bcol, kh * (bcol * eg)], axis=1)
            uw = _dot(pmat, rhs)
            u = uw[:, 0:DN_HEAD_DIM]
            w = uw[:, DN_HEAD_DIM:2 * DN_HEAD_DIM]
            qdec = qh * eg
            kdec = kh * jnp.exp(glast - gcol)
            s = st_s[h]
            ws = _dot(jnp.concatenate([w, qdec], axis=0), s)
            vnew = u - ws[0:CHUNK]
            o = ws[CHUNK:2 * CHUNK] + _dot(attn, vnew)
            st_s[h] = s * jnp.exp(glast) + _dot_tn(kdec, vnew)
            on = o * lax.rsqrt(jnp.mean(o * o, axis=-1, keepdims=True) + EPS)
            z = za_ref[0, pl.ds(r0, CHUNK), lo:hi]
            o_ref[0, pl.ds(r0, CHUNK), lo:hi] = on * nw_ref[...] * _silu(z)
        return carry

    lax.fori_loop(0, nchunk, chunk_body, 0)


def _delta(qkv, za, ba, conv_w, gpar, norm_w):
    b, seq, _ = za.shape
    lblk = DELTA_LBLK
    blk = lambda j: pl.BlockSpec((1, lblk, DN_WIDTH), lambda bi, li: (bi, li, j))
    cw = lambda j: pl.BlockSpec((CONV_K, DN_WIDTH), lambda bi, li: (0, j))
    return pl.pallas_call(
        _delta_kernel,
        grid=(b, seq // lblk),
        in_specs=[blk(0), blk(1), blk(2), blk(0),
                  pl.BlockSpec((1, lblk, LANES), lambda bi, li: (bi, li, 0)),
                  cw(0), cw(1), cw(2),
                  pl.BlockSpec((8, LANES), lambda bi, li: (0, 0)),
                  pl.BlockSpec((1, DN_HEAD_DIM), lambda bi, li: (0, 0))],
        out_specs=blk(0),
        out_shape=jax.ShapeDtypeStruct((b, seq, DN_WIDTH), F32),
        scratch_shapes=[pltpu.VMEM((lblk + PAD, DN_WIDTH), F32)] * 3
        + [pltpu.VMEM((lblk, DN_WIDTH), F32)] * 3
        + [pltpu.VMEM((DN_HEADS, DN_HEAD_DIM, DN_HEAD_DIM), F32),
           pltpu.VMEM((lblk, LANES), F32),
           pltpu.VMEM((lblk, LANES), F32),
           pltpu.VMEM((lblk // CHUNK, LANES, LANES), F32)],
        compiler_params=_cparams(("arbitrary", "arbitrary")),
        name="delta",
    )(qkv, qkv, qkv, za, ba, conv_w, conv_w, conv_w, gpar, norm_w)


def _s5_kernel(u_ref, mi_ref, min_ref, mout_ref, pw_ref, dv_ref, y_ref, *, rows_per_batch):
    u = u_ref[0]
    rtot = u.shape[0]
    ub = u.astype(BF16)
    x = jnp.dot(ub, min_ref[0], preferred_element_type=F32)
    rowi = lax.broadcasted_iota(jnp.int32, (rtot, 2 * S5_STATE), 0) & (rows_per_batch - 1)
    d = 1
    k = 0
    while d < rows_per_batch:
        sh = jnp.where(rowi >= d, pltpu.roll(x, d, axis=0), 0.0)
        x = x + sh * pw_ref[0, 2 * k:2 * k + 1, :] + pltpu.roll(sh, S5_STATE, axis=1) * pw_ref[0, 2 * k + 1:2 * k + 2, :]
        d *= 2
        k += 1
    xprev = jnp.where(rowi >= 1, pltpu.roll(x, 1, axis=0), 0.0)
    y = jnp.dot(ub, mi_ref[0], preferred_element_type=F32)
    y = y + lax.dot_general(xprev.astype(BF16), mout_ref[0], (((1,), (1,)), ((), ())),
                            preferred_element_type=F32)
    y_ref[0] = y + u * dv_ref[0]


def _s5(ut, mintra, minp, moutt, pw, dvec, rows_per_batch):
    g, r, tc = ut.shape
    per = lambda shape: pl.BlockSpec((1,) + shape, lambda gi: (gi, 0, 0))
    return pl.pallas_call(
        functools.partial(_s5_kernel, rows_per_batch=rows_per_batch),
        grid=(g,),
        in_specs=[per((r, tc)), per((tc, tc)), per((tc, 2 * S5_STATE)), per((tc, 2 * S5_STATE)),
                  per(pw.shape[1:]), per((1, tc))],
        out_specs=per((r, tc)),
        out_shape=jax.ShapeDtypeStruct((g, r, tc), F32),
        compiler_params=_cparams(("arbitrary",)),
        name="s5",
    )(ut, mintra, minp, moutt, pw, dvec)


def _s5_weights(lam_re, lam_im, log_dt, b_re, b_im, c_re, c_im, d_skip, nsteps):
    t = S5_T
    dt = jnp.exp(log_dt)[:, None]
    lr = jnp.minimum(lam_re, -1e-4)
    li = lam_im

    def lam_pow(n):
        mag = jnp.exp(lr * dt * n)
        return mag * jnp.cos(li * dt * n), mag * jnp.sin(li * dt * n)

    ab_re, ab_im = lam_pow(1.0)
    den = lr * lr + li * li
    f_re = ((ab_re - 1.0) * lr + ab_im * li) / den
    f_im = (ab_im * lr - (ab_re - 1.0) * li) / den
    bb_re = f_re[..., None] * b_re - f_im[..., None] * b_im
    bb_im = f_re[..., None] * b_im + f_im[..., None] * b_re
    tau = jnp.arange(t + 1, dtype=F32)[:, None, None]
    pr, pi = lam_pow(tau)
    hi = lax.Precision.HIGHEST
    lb_re = pr[:t, :, :, None] * bb_re[None] - pi[:t, :, :, None] * bb_im[None]
    lb_im = pr[:t, :, :, None] * bb_im[None] + pi[:t, :, :, None] * bb_re[None]
    kt = (jnp.einsum('gop,tgpc->tgoc', c_re, lb_re, precision=hi)
          - jnp.einsum('gop,tgpc->tgoc', c_im, lb_im, precision=hi))
    ii = jnp.arange(t)[:, None]
    jj = jnp.arange(t)[None, :]
    lag = jnp.clip(jj - ii, 0, t - 1)
    kij = jnp.where((jj >= ii)[:, :, None, None, None], kt[lag], 0.0)
    mintra = jnp.transpose(kij, (2, 0, 4, 1, 3)).reshape(S5_GROUPS, S5_TC, S5_TC)
    rev = t - 1 - jnp.arange(t)
    min_re = jnp.transpose(lb_re[rev], (1, 0, 3, 2)).reshape(S5_GROUPS, S5_TC, S5_STATE)
    min_im = jnp.transpose(lb_im[rev], (1, 0, 3, 2)).reshape(S5_GROUPS, S5_TC, S5_STATE)
    minp = jnp.concatenate([min_re, min_im], axis=-1)
    cl_re = c_re[None] * pr[1:, :, None, :] - c_im[None] * pi[1:, :, None, :]
    cl_im = c_re[None] * pi[1:, :, None, :] + c_im[None] * pr[1:, :, None, :]
    moutt = jnp.concatenate([jnp.transpose(cl_re, (1, 0, 2, 3)), -jnp.transpose(cl_im, (1, 0, 2, 3))],
                            axis=-1).reshape(S5_GROUPS, S5_TC, 2 * S5_STATE)
    steps = (t * (2.0 ** jnp.arange(nsteps, dtype=F32)))[:, None, None]
    sr, si = lam_pow(steps)
    pw = jnp.stack([jnp.concatenate([sr, sr], -1), jnp.concatenate([-si, si], -1)], axis=1)
    pw = jnp.transpose(pw, (2, 0, 1, 3)).reshape(S5_GROUPS, 2 * nsteps, 2 * S5_STATE)
    pad = (-pw.shape[1]) % 8
    pw = jnp.pad(pw, ((0, 0), (0, pad), (0, 0)))
    dvec = jnp.tile(d_skip.reshape(S5_GROUPS, 1, S5_GROUP), (1, t, 1)).reshape(S5_GROUPS, 1, S5_TC)
    return mintra.astype(BF16), minp.astype(BF16), moutt.astype(BF16), pw, dvec


FINAL_TM = 256


def _final_kernel(x_ref, og_ref, ys_ref, zb_ref, ga_ref, gb_ref, mod_ref,
                  wpa_ref, glw_ref, glb_ref, wpb_ref, wo_ref, fnw_ref, out_ref):
    ya = jnp.dot(og_ref[...].astype(BF16), wpa_ref[...], preferred_element_type=F32)
    ys = ys_ref[...]
    gy = 0.5 * ys * (1.0 + jnp.tanh(math.sqrt(2.0 / math.pi) * (ys + 0.044715 * (ys * ys * ys))))
    glu = jnp.dot(gy.astype(BF16), glw_ref[...], preferred_element_type=F32) + glb_ref[...]
    y2 = gy * _sigmoid(glu)
    yb = jnp.dot((y2 * _silu(zb_ref[...])).astype(BF16), wpb_ref[...], preferred_element_type=F32)
    merged = _sigmoid(ga_ref[...]) * ya + _sigmoid(gb_ref[...]) * yb
    mo = jnp.dot(merged.astype(BF16), wo_ref[...], preferred_element_type=F32)
    gate = mod_ref[0, :, 2 * D_MODEL:3 * D_MODEL]
    xo = x_ref[...] + gate * mo
    out_ref[...] = xo * lax.rsqrt(jnp.mean(xo * xo, axis=-1, keepdims=True) + EPS) * fnw_ref[...]


def _final(x2, og, ys, zb, ga, gb, mod3, wpa, glw, glb, wpb, wo, fnw, seq):
    n = x2.shape[0]
    tm = FINAL_TM
    per_b = seq // tm
    row = lambda w: pl.BlockSpec((tm, w), lambda i: (i, 0))
    const = lambda a: pl.BlockSpec(a.shape, lambda i: (0,) * a.ndim)
    return pl.pallas_call(
        _final_kernel,
        grid=(n // tm,),
        in_specs=[row(D_MODEL), row(DN_WIDTH), row(S5_WIDTH), row(S5_WIDTH), row(D_MODEL), row(D_MODEL),
                  pl.BlockSpec((1, 1, 3 * D_MODEL), lambda i: (i // per_b, 0, 0)),
                  const(wpa), const(glw), const(glb), const(wpb), const(wo), const(fnw)],
        out_specs=row(D_MODEL),
        out_shape=jax.ShapeDtypeStruct((n, D_MODEL), F32),
        compiler_params=_cparams(("arbitrary",)),
        name="final",
    )(x2, og, ys, zb, ga, gb, mod3, wpa, glw, glb, wpb, wo, fnw)


def kernel(x, c, w_ada, b_ada, w_in, dn_conv_w, dn_a_log, dn_dt_bias, dn_norm_w, w_proj_a,
           s5_lambda_re, s5_lambda_im, s5_log_dt, s5_b_re, s5_b_im, s5_c_re, s5_c_im, s5_d,
           s5_glu_w, s5_glu_b, w_proj_b, w_out, final_norm_w):
    bsz, seq, d = x.shape
    n = bsz * seq
    assert w_ada.shape[0] == 1, "the final rmsnorm is fused into the single layer's epilogue"
    xcur = x
    for layer in range(1):
        x2 = xcur.reshape(n, d)
        c8 = jnp.zeros((8, d), F32).at[:bsz].set(c)
        mod = _ada(c8, w_ada[layer], b_ada[layer][None, :])[:bsz]
        mod3 = mod.reshape(bsz, 1, 3 * d)

        wl = w_in[layer]
        o_ba = 4 * DN_WIDTH
        o_rest = o_ba + 2 * DN_HEADS
        w1 = wl[:, :o_ba].astype(BF16)
        wba = jnp.pad(wl[:, o_ba:o_rest], ((0, 0), (0, LANES - 2 * DN_HEADS))).astype(BF16)
        w2 = wl[:, o_rest:].astype(BF16)
        qkv, za, ba, u, zb, ga, gb = _inproj(x2, mod3, w1, wba, w2, seq)

        gpar = jnp.zeros((8, LANES), F32)
        gpar = gpar.at[0, DN_HEADS:2 * DN_HEADS].set(dn_a_log[layer])
        gpar = gpar.at[1, DN_HEADS:2 * DN_HEADS].set(dn_dt_bias[layer])
        og = _delta(qkv.reshape(bsz, seq, 3 * DN_WIDTH), za.reshape(bsz, seq, DN_WIDTH),
                    ba.reshape(bsz, seq, LANES), dn_conv_w[layer], gpar, dn_norm_w[layer][None, :])

        rows_per_batch = seq // S5_T
        nsteps = int(math.log2(rows_per_batch))
        mintra, minp, moutt, pw, dvec = _s5_weights(
            s5_lambda_re[layer], s5_lambda_im[layer], s5_log_dt[layer], s5_b_re[layer], s5_b_im[layer],
            s5_c_re[layer], s5_c_im[layer], s5_d[layer], nsteps)
        ut = u.reshape(n // S5_T, S5_T, S5_GROUPS, S5_GROUP).transpose(2, 0, 1, 3).reshape(
            S5_GROUPS, n // S5_T, S5_TC)
        yt = _s5(ut, mintra, minp, moutt, pw, dvec, rows_per_batch)
        ys = yt.reshape(S5_GROUPS, n // S5_T, S5_T, S5_GROUP).transpose(1, 2, 0, 3).reshape(n, S5_WIDTH)

        out = _final(x2, og.reshape(n, DN_WIDTH), ys, zb, ga, gb, mod3,
                     w_proj_a[layer].astype(BF16), s5_glu_w[layer].astype(BF16), s5_glu_b[layer][None, :],
                     w_proj_b[layer].astype(BF16), w_out[layer].astype(BF16), final_norm_w[None, :], seq)
        xcur = out.reshape(bsz, seq, d)
    return xcur
```

```python
import functools
import math

import jax
import jax.numpy as jnp
import numpy as np
from jax import lax
from jax.experimental import pallas as pl
from jax.experimental.pallas import tpu as pltpu

D_MODEL = 1024
DN_HEADS = 8
DN_HEAD_DIM = 128
DN_WIDTH = DN_HEADS * DN_HEAD_DIM
CONV_K = 4
CHUNK = 64
S5_WIDTH = D_MODEL // 2
S5_GROUP = 16
S5_GROUPS = S5_WIDTH // S5_GROUP
S5_STATE = 64
S5_T = 16
S5_TC = S5_T * S5_GROUP
EPS = 1e-6
LANES = 128
VMEM_LIMIT = 56 * 1024 * 1024

BF16 = jnp.bfloat16
F32 = jnp.float32


def _cparams(sem):
    return pltpu.CompilerParams(dimension_semantics=sem, vmem_limit_bytes=VMEM_LIMIT)


def _sigmoid(x):
    return 1.0 / (1.0 + jnp.exp(-x))


def _silu(x):
    return x * _sigmoid(x)


def _dot(a, b):
    return jnp.dot(a.astype(BF16), b.astype(BF16), preferred_element_type=F32)


def _dot_nt(a, b):
    return lax.dot_general(a.astype(BF16), b.astype(BF16), (((1,), (1,)), ((), ())),
                           preferred_element_type=F32)


def _dot_tn(a, b):
    return lax.dot_general(a.astype(BF16), b.astype(BF16), (((0,), (0,)), ((), ())),
                           preferred_element_type=F32)


def _ada_kernel(c_ref, w_ref, b_ref, o_ref):
    o_ref[...] = jnp.dot(c_ref[...], w_ref[...], preferred_element_type=F32,
                         precision=lax.Precision.HIGHEST) + b_ref[...]


def _ada(c8, w_ada, b_ada):
    n3 = w_ada.shape[1]
    tn = 1024
    return pl.pallas_call(
        _ada_kernel,
        grid=(n3 // tn,),
        in_specs=[pl.BlockSpec((8, D_MODEL), lambda j: (0, 0)),
                  pl.BlockSpec((D_MODEL, tn), lambda j: (0, j)),
                  pl.BlockSpec((1, tn), lambda j: (0, j))],
        out_specs=pl.BlockSpec((8, tn), lambda j: (0, j)),
        out_shape=jax.ShapeDtypeStruct((8, n3), F32),
        compiler_params=_cparams(("arbitrary",)),
        name="ada",
    )(c8, w_ada, b_ada)


INPROJ_TM = 256
INPROJ_TN = 512


def _inproj_kernel(x_ref, mod_ref, w1_ref, wba_ref, w2_ref,
                   qkv_ref, za_ref, ba_ref, u_ref, zb_ref, ga_ref, gb_ref):
    x = x_ref[...]
    xn = x * lax.rsqrt(jnp.mean(x * x, axis=-1, keepdims=True) + EPS)
    shift = mod_ref[0, :, 0:D_MODEL]
    scale = mod_ref[0, :, D_MODEL:2 * D_MODEL]
    h = (xn * (1.0 + scale) + shift).astype(BF16)

    def cols(w_ref, lo, out_ref, width):
        for j in range(width // INPROJ_TN):
            out_ref[:, j * INPROJ_TN:(j + 1) * INPROJ_TN] = jnp.dot(
                h, w_ref[:, lo + j * INPROJ_TN: lo + (j + 1) * INPROJ_TN],
                preferred_element_type=F32)

    cols(w1_ref, 0, qkv_ref, 3 * DN_WIDTH)
    cols(w1_ref, 3 * DN_WIDTH, za_ref, DN_WIDTH)
    ba_ref[...] = jnp.dot(h, wba_ref[...], preferred_element_type=F32)
    cols(w2_ref, 0, u_ref, S5_WIDTH)
    cols(w2_ref, S5_WIDTH, zb_ref, S5_WIDTH)
    cols(w2_ref, 2 * S5_WIDTH, ga_ref, D_MODEL)
    cols(w2_ref, 2 * S5_WIDTH + D_MODEL, gb_ref, D_MODEL)


def _inproj(x2, mod3, w1, wba, w2, seq):
    n = x2.shape[0]
    tm = INPROJ_TM
    per_b = seq // tm
    row = lambda i: (i, 0)
    const = lambda i: (0, 0)
    widths = [3 * DN_WIDTH, DN_WIDTH, LANES, S5_WIDTH, S5_WIDTH, D_MODEL, D_MODEL]
    return pl.pallas_call(
        _inproj_kernel,
        grid=(n // tm,),
        in_specs=[pl.BlockSpec((tm, D_MODEL), row),
                  pl.BlockSpec((1, 1, 3 * D_MODEL), lambda i: (i // per_b, 0, 0)),
                  pl.BlockSpec(w1.shape, const),
                  pl.BlockSpec(wba.shape, const),
                  pl.BlockSpec(w2.shape, const)],
        out_specs=[pl.BlockSpec((tm, w), row) for w in widths],
        out_shape=[jax.ShapeDtypeStruct((n, w), F32) for w in widths],
        compiler_params=_cparams(("arbitrary",)),
        name="inproj",
    )(x2, mod3, w1, wba, w2)


DELTA_LBLK = 256
DELTA_NC = DELTA_LBLK // CHUNK
DELTA_NI = DN_HEADS * DELTA_NC
DELTA_R = DELTA_NI * CHUNK
PAD = 8


def _delta_kernel(q_ref, k_ref, v_ref, za_ref, ba_ref, cwq_ref, cwk_ref, cwv_ref, gpar_ref, nw_ref, o_ref,
                  pq_s, pk_s, pv_s, q_s, k_s, v_s, st_s, gcol_s, beta_s, glast_s, grow_s,
                  qk_s, am_s, t_s, x_s, wu_s, mq_s, n_s, o0_s, oo_s):
    lblk, nc, dh = DELTA_LBLK, DELTA_NC, DN_HEAD_DIM
    insts = [(h, c) for h in range(DN_HEADS) for c in range(nc)]

    def rows(h, c):
        r0 = (h * nc + c) * CHUNK
        return slice(r0, r0 + CHUNK)

    @pl.when(pl.program_id(1) == 0)
    def _():
        zero = jnp.zeros((PAD, DN_WIDTH), F32)
        pq_s[0:PAD, :] = zero
        pk_s[0:PAD, :] = zero
        pv_s[0:PAD, :] = zero
        st_s[...] = jnp.zeros_like(st_s)

    def conv(src_ref, pad_s, cw_ref, dst_s, l2, post_scale):
        pad_s[PAD:PAD + lblk, :] = src_ref[0]
        acc = pad_s[PAD:PAD + lblk, :] * cw_ref[CONV_K - 1:CONV_K, :]
        for j in range(CONV_K - 1):
            sh = CONV_K - 1 - j
            acc = acc + pad_s[PAD - sh:PAD - sh + lblk, :] * cw_ref[j:j + 1, :]
        y = _silu(acc)
        pad_s[0:PAD, :] = pad_s[lblk:lblk + PAD, :]
        for h in range(DN_HEADS):
            yh = y[:, h * dh:(h + 1) * dh]
            if l2:
                yh = yh * (lax.rsqrt(jnp.sum(yh * yh, axis=-1, keepdims=True) + EPS) * post_scale)
            dst_s[h * lblk:(h + 1) * lblk, :] = yh

    conv(q_ref, pq_s, cwq_ref, q_s, True, dh ** -0.5)
    conv(k_ref, pk_s, cwk_ref, k_s, True, 1.0)
    conv(v_ref, pv_s, cwv_ref, v_s, False, 1.0)

    ba = ba_ref[0]
    beta = _sigmoid(ba)
    zg = ba + gpar_ref[1:2, :]
    softplus = jnp.maximum(zg, 0.0) + jnp.log(1.0 + jnp.exp(-jnp.abs(zg)))
    g = -jnp.exp(gpar_ref[0:1, :]) * softplus
    rowi = lax.broadcasted_iota(jnp.int32, (lblk, LANES), 0) & (CHUNK - 1)
    d = 1
    while d < CHUNK:
        g = g + jnp.where(rowi >= d, pltpu.roll(g, d, axis=0), 0.0)
        d *= 2
    for h in range(DN_HEADS):
        gcol_s[h * lblk:(h + 1) * lblk, :] = jnp.broadcast_to(g[:, DN_HEADS + h:DN_HEADS + h + 1], (lblk, dh))
        beta_s[h * lblk:(h + 1) * lblk, :] = jnp.broadcast_to(beta[:, h:h + 1], (lblk, dh))
    for c in range(nc):
        gct = jnp.concatenate([g[c * CHUNK:(c + 1) * CHUNK, :], jnp.zeros((LANES - CHUNK, LANES), F32)], axis=0).T
        for h in range(DN_HEADS):
            grow_s[rows(h, c), :] = jnp.broadcast_to(gct[DN_HEADS + h:DN_HEADS + h + 1, 0:CHUNK], (CHUNK, CHUNK))
            glast_s[rows(h, c), :] = jnp.broadcast_to(
                g[(c + 1) * CHUNK - 1:(c + 1) * CHUNK, DN_HEADS + h:DN_HEADS + h + 1], (CHUNK, dh))

    for h, c in insts:
        r = rows(h, c)
        kq = _dot_nt(jnp.concatenate([q_s[r, :], k_s[r, :]], axis=0), k_s[r, :])
        qk_s[r, :] = kq[0:CHUNK]
        am_s[r, :] = kq[CHUNK:2 * CHUNK]

    ri = lax.broadcasted_iota(jnp.int32, (DELTA_R, CHUNK), 0) & (CHUNK - 1)
    ci = lax.broadcasted_iota(jnp.int32, (DELTA_R, CHUNK), 1)
    dec = jnp.where(ri >= ci, jnp.exp(jnp.minimum(gcol_s[:, 0:CHUNK] - grow_s[...], 0.0)), 0.0)
    qk_s[...] = qk_s[...] * dec
    amat = jnp.where(ri > ci, beta_s[:, 0:CHUNK] * am_s[...] * dec, 0.0)
    am_s[...] = amat
    t_s[...] = jnp.where(ri == ci, 1.0, 0.0) - jnp.where((ri >> 1) == (ci >> 1), amat, 0.0)
    r1 = lax.broadcasted_iota(jnp.int32, (CHUNK, CHUNK), 0)
    c1 = lax.broadcasted_iota(jnp.int32, (CHUNK, CHUNK), 1)
    lvl = 1
    while (1 << lvl) < CHUNK:
        off_mask = ((r1 >> (lvl + 1)) == (c1 >> (lvl + 1))) & ((r1 >> lvl) != (c1 >> lvl))
        for h, c in insts:
            r = rows(h, c)
            x_s[r, :] = _dot(jnp.where(off_mask, am_s[r, :], 0.0), t_s[r, :])
        for h, c in insts:
            r = rows(h, c)
            t_s[r, :] = t_s[r, :] - _dot(t_s[r, :], x_s[r, :])
        lvl += 1

    eg = jnp.exp(gcol_s[...])
    wu_s[:, 0:dh] = k_s[...] * (beta_s[...] * eg)
    wu_s[:, dh:2 * dh] = v_s[...] * beta_s[...]
    for h, c in insts:
        r = rows(h, c)
        wu_s[r, :] = _dot(t_s[r, :], wu_s[r, :])

    q_s[...] = q_s[...] * eg
    k_s[...] = k_s[...] * jnp.exp(glast_s[...] - gcol_s[...])
    for i, (h, c) in enumerate(insts):
        r = rows(h, c)
        wu = wu_s[r, :]
        aw = _dot(qk_s[r, :], wu)
        mn = _dot_tn(k_s[r, :], wu)
        mq_s[i, 0:dh, :] = (-mn[:, 0:dh]).astype(BF16)
        mq_s[i, dh:dh + CHUNK, :] = (q_s[r, :] - aw[:, 0:dh]).astype(BF16)
        n_s[i] = mn[:, dh:2 * dh]
        o0_s[r, :] = aw[:, dh:2 * dh]

    for c in range(nc):
        for h in range(DN_HEADS):
            i = h * nc + c
            r = rows(h, c)
            s = st_s[h]
            res = jnp.dot(mq_s[i], s.astype(BF16), preferred_element_type=F32)
            gl = jnp.exp(glast_s[r.start:r.start + 1, :])
            st_s[h] = s * gl + res[0:dh] + n_s[i]
            oo_s[r, :] = res[dh:dh + CHUNK] + o0_s[r, :]

    o = oo_s[...]
    on = o * lax.rsqrt(jnp.mean(o * o, axis=-1, keepdims=True) + EPS) * nw_ref[...]
    for h in range(DN_HEADS):
        o_ref[0, :, h * dh:(h + 1) * dh] = on[h * lblk:(h + 1) * lblk, :] * _silu(za_ref[0, :, h * dh:(h + 1) * dh])


def _delta(qkv, za, ba, conv_w, gpar, norm_w):
    b, seq, _ = za.shape
    lblk = DELTA_LBLK
    blk = lambda j: pl.BlockSpec((1, lblk, DN_WIDTH), lambda bi, li: (bi, li, j))
    cw = lambda j: pl.BlockSpec((CONV_K, DN_WIDTH), lambda bi, li: (0, j))
    wide = pltpu.VMEM((DELTA_R, DN_HEAD_DIM), F32)
    narrow = pltpu.VMEM((DELTA_R, CHUNK), F32)
    return pl.pallas_call(
        _delta_kernel,
        grid=(b, seq // lblk),
        in_specs=[blk(0), blk(1), blk(2), blk(0),
                  pl.BlockSpec((1, lblk, LANES), lambda bi, li: (bi, li, 0)),
                  cw(0), cw(1), cw(2),
                  pl.BlockSpec((8, LANES), lambda bi, li: (0, 0)),
                  pl.BlockSpec((1, DN_HEAD_DIM), lambda bi, li: (0, 0))],
        out_specs=blk(0),
        out_shape=jax.ShapeDtypeStruct((b, seq, DN_WIDTH), F32),
        scratch_shapes=[pltpu.VMEM((lblk + PAD, DN_WIDTH), F32)] * 3
        + [wide] * 3
        + [pltpu.VMEM((DN_HEADS, DN_HEAD_DIM, DN_HEAD_DIM), F32)]
        + [wide] * 3
        + [narrow] * 5
        + [pltpu.VMEM((DELTA_R, 2 * DN_HEAD_DIM), F32),
           pltpu.VMEM((DELTA_NI, DN_HEAD_DIM + CHUNK, DN_HEAD_DIM), BF16),
           pltpu.VMEM((DELTA_NI, DN_HEAD_DIM, DN_HEAD_DIM), F32),
           wide, wide],
        compiler_params=_cparams(("arbitrary", "arbitrary")),
        name="delta",
    )(qkv, qkv, qkv, za, ba, conv_w, conv_w, conv_w, gpar, norm_w)


def _s5_kernel(u_ref, mi_ref, min_ref, mout_ref, pw_ref, dv_ref, y_ref, *, rows_per_batch):
    u = u_ref[0]
    rtot = u.shape[0]
    ub = u.astype(BF16)
    x = jnp.dot(ub, min_ref[0], preferred_element_type=F32)
    rowi = lax.broadcasted_iota(jnp.int32, (rtot, 2 * S5_STATE), 0) & (rows_per_batch - 1)
    d = 1
    k = 0
    while d < rows_per_batch:
        sh = jnp.where(rowi >= d, pltpu.roll(x, d, axis=0), 0.0)
        x = x + sh * pw_ref[0, 2 * k:2 * k + 1, :] + pltpu.roll(sh, S5_STATE, axis=1) * pw_ref[0, 2 * k + 1:2 * k + 2, :]
        d *= 2
        k += 1
    xprev = jnp.where(rowi >= 1, pltpu.roll(x, 1, axis=0), 0.0)
    y = jnp.dot(ub, mi_ref[0], preferred_element_type=F32)
    y = y + lax.dot_general(xprev.astype(BF16), mout_ref[0], (((1,), (1,)), ((), ())),
                            preferred_element_type=F32)
    y_ref[0] = y + u * dv_ref[0]


def _s5(ut, mintra, minp, moutt, pw, dvec, rows_per_batch):
    g, r, tc = ut.shape
    per = lambda shape: pl.BlockSpec((1,) + shape, lambda gi: (gi, 0, 0))
    return pl.pallas_call(
        functools.partial(_s5_kernel, rows_per_batch=rows_per_batch),
        grid=(g,),
        in_specs=[per((r, tc)), per((tc, tc)), per((tc, 2 * S5_STATE)), per((tc, 2 * S5_STATE)),
                  per(pw.shape[1:]), per((1, tc))],
        out_specs=per((r, tc)),
        out_shape=jax.ShapeDtypeStruct((g, r, tc), F32),
        compiler_params=_cparams(("arbitrary",)),
        name="s5",
    )(ut, mintra, minp, moutt, pw, dvec)


def _s5_weights(lam_re, lam_im, log_dt, b_re, b_im, c_re, c_im, d_skip, nsteps):
    t = S5_T
    dt = jnp.exp(log_dt)[:, None]
    lr = jnp.minimum(lam_re, -1e-4)
    li = lam_im

    def lam_pow(n):
        mag = jnp.exp(lr * dt * n)
        return mag * jnp.cos(li * dt * n), mag * jnp.sin(li * dt * n)

    ab_re, ab_im = lam_pow(1.0)
    den = lr * lr + li * li
    f_re = ((ab_re - 1.0) * lr + ab_im * li) / den
    f_im = (ab_im * lr - (ab_re - 1.0) * li) / den
    bb_re = f_re[..., None] * b_re - f_im[..., None] * b_im
    bb_im = f_re[..., None] * b_im + f_im[..., None] * b_re
    tau = jnp.arange(t + 1, dtype=F32)[:, None, None]
    pr, pi = lam_pow(tau)
    hi = lax.Precision.HIGHEST
    lb_re = pr[:t, :, :, None] * bb_re[None] - pi[:t, :, :, None] * bb_im[None]
    lb_im = pr[:t, :, :, None] * bb_im[None] + pi[:t, :, :, None] * bb_re[None]
    kt = (jnp.einsum('gop,tgpc->tgoc', c_re, lb_re, precision=hi)
          - jnp.einsum('gop,tgpc->tgoc', c_im, lb_im, precision=hi))
    ii = jnp.arange(t)[:, None]
    jj = jnp.arange(t)[None, :]
    lag = jnp.clip(jj - ii, 0, t - 1)
    kij = jnp.where((jj >= ii)[:, :, None, None, None], kt[lag], 0.0)
    mintra = jnp.transpose(kij, (2, 0, 4, 1, 3)).reshape(S5_GROUPS, S5_TC, S5_TC)
    rev = t - 1 - jnp.arange(t)
    min_re = jnp.transpose(lb_re[rev], (1, 0, 3, 2)).reshape(S5_GROUPS, S5_TC, S5_STATE)
    min_im = jnp.transpose(lb_im[rev], (1, 0, 3, 2)).reshape(S5_GROUPS, S5_TC, S5_STATE)
    minp = jnp.concatenate([min_re, min_im], axis=-1)
    cl_re = c_re[None] * pr[1:, :, None, :] - c_im[None] * pi[1:, :, None, :]
    cl_im = c_re[None] * pi[1:, :, None, :] + c_im[None] * pr[1:, :, None, :]
    moutt = jnp.concatenate([jnp.transpose(cl_re, (1, 0, 2, 3)), -jnp.transpose(cl_im, (1, 0, 2, 3))],
                            axis=-1).reshape(S5_GROUPS, S5_TC, 2 * S5_STATE)
    steps = (t * (2.0 ** jnp.arange(nsteps, dtype=F32)))[:, None, None]
    sr, si = lam_pow(steps)
    pw = jnp.stack([jnp.concatenate([sr, sr], -1), jnp.concatenate([-si, si], -1)], axis=1)
    pw = jnp.transpose(pw, (2, 0, 1, 3)).reshape(S5_GROUPS, 2 * nsteps, 2 * S5_STATE)
    pad = (-pw.shape[1]) % 8
    pw = jnp.pad(pw, ((0, 0), (0, pad), (0, 0)))
    dvec = jnp.tile(d_skip.reshape(S5_GROUPS, 1, S5_GROUP), (1, t, 1)).reshape(S5_GROUPS, 1, S5_TC)
    return mintra.astype(BF16), minp.astype(BF16), moutt.astype(BF16), pw, dvec


FINAL_TM = 256


def _final_kernel(x_ref, og_ref, ys_ref, zb_ref, ga_ref, gb_ref, mod_ref,
                  wpa_ref, glw_ref, glb_ref, wpb_ref, wo_ref, fnw_ref, out_ref):
    ya = jnp.dot(og_ref[...].astype(BF16), wpa_ref[...], preferred_element_type=F32)
    ys = ys_ref[...]
    gy = 0.5 * ys * (1.0 + jnp.tanh(math.sqrt(2.0 / math.pi) * (ys + 0.044715 * (ys * ys * ys))))
    glu = jnp.dot(gy.astype(BF16), glw_ref[...], preferred_element_type=F32) + glb_ref[...]
    y2 = gy * _sigmoid(glu)
    yb = jnp.dot((y2 * _silu(zb_ref[...])).astype(BF16), wpb_ref[...], preferred_element_type=F32)
    merged = _sigmoid(ga_ref[...]) * ya + _sigmoid(gb_ref[...]) * yb
    mo = jnp.dot(merged.astype(BF16), wo_ref[...], preferred_element_type=F32)
    gate = mod_ref[0, :, 2 * D_MODEL:3 * D_MODEL]
    xo = x_ref[...] + gate * mo
    out_ref[...] = xo * lax.rsqrt(jnp.mean(xo * xo, axis=-1, keepdims=True) + EPS) * fnw_ref[...]


def _final(x2, og, ys, zb, ga, gb, mod3, wpa, glw, glb, wpb, wo, fnw, seq):
    n = x2.shape[0]
    tm = FINAL_TM
    per_b = seq // tm
    row = lambda w: pl.BlockSpec((tm, w), lambda i: (i, 0))
    const = lambda a: pl.BlockSpec(a.shape, lambda i: (0,) * a.ndim)
    return pl.pallas_call(
        _final_kernel,
        grid=(n // tm,),
        in_specs=[row(D_MODEL), row(DN_WIDTH), row(S5_WIDTH), row(S5_WIDTH), row(D_MODEL), row(D_MODEL),
                  pl.BlockSpec((1, 1, 3 * D_MODEL), lambda i: (i // per_b, 0, 0)),
                  const(wpa), const(glw), const(glb), const(wpb), const(wo), const(fnw)],
        out_specs=row(D_MODEL),
        out_shape=jax.ShapeDtypeStruct((n, D_MODEL), F32),
        compiler_params=_cparams(("arbitrary",)),
        name="final",
    )(x2, og, ys, zb, ga, gb, mod3, wpa, glw, glb, wpb, wo, fnw)


def kernel(x, c, w_ada, b_ada, w_in, dn_conv_w, dn_a_log, dn_dt_bias, dn_norm_w, w_proj_a,
           s5_lambda_re, s5_lambda_im, s5_log_dt, s5_b_re, s5_b_im, s5_c_re, s5_c_im, s5_d,
           s5_glu_w, s5_glu_b, w_proj_b, w_out, final_norm_w):
    bsz, seq, d = x.shape
    n = bsz * seq
    assert w_ada.shape[0] == 1, "the final rmsnorm is fused into the single layer's epilogue"
    layer = 0
    x2 = x.reshape(n, d)
    c8 = jnp.zeros((8, d), F32).at[:bsz].set(c)
    mod = _ada(c8, w_ada[layer], b_ada[layer][None, :])[:bsz]
    mod3 = mod.reshape(bsz, 1, 3 * d)

    wl = w_in[layer]
    o_ba = 4 * DN_WIDTH
    o_rest = o_ba + 2 * DN_HEADS
    w1 = wl[:, :o_ba].astype(BF16)
    wba = jnp.pad(wl[:, o_ba:o_rest], ((0, 0), (0, LANES - 2 * DN_HEADS))).astype(BF16)
    w2 = wl[:, o_rest:].astype(BF16)
    qkv, za, ba, u, zb, ga, gb = _inproj(x2, mod3, w1, wba, w2, seq)

    gpar = jnp.zeros((8, LANES), F32)
    gpar = gpar.at[0, DN_HEADS:2 * DN_HEADS].set(dn_a_log[layer])
    gpar = gpar.at[1, DN_HEADS:2 * DN_HEADS].set(dn_dt_bias[layer])
    og = _delta(qkv.reshape(bsz, seq, 3 * DN_WIDTH), za.reshape(bsz, seq, DN_WIDTH),
                ba.reshape(bsz, seq, LANES), dn_conv_w[layer], gpar, dn_norm_w[layer][None, :])

    rows_per_batch = seq // S5_T
    nsteps = int(math.log2(rows_per_batch))
    mintra, minp, moutt, pw, dvec = _s5_weights(
        s5_lambda_re[layer], s5_lambda_im[layer], s5_log_dt[layer], s5_b_re[layer], s5_b_im[layer],
        s5_c_re[layer], s5_c_im[layer], s5_d[layer], nsteps)
    ut = u.reshape(n // S5_T, S5_T, S5_GROUPS, S5_GROUP).transpose(2, 0, 1, 3).reshape(
        S5_GROUPS, n // S5_T, S5_TC)
    yt = _s5(ut, mintra, minp, moutt, pw, dvec, rows_per_batch)
    ys = yt.reshape(S5_GROUPS, n // S5_T, S5_T, S5_GROUP).transpose(1, 2, 0, 3).reshape(n, S5_WIDTH)

    out = _final(x2, og.reshape(n, DN_WIDTH), ys, zb, ga, gb, mod3,
                 w_proj_a[layer].astype(BF16), s5_glu_w[layer].astype(BF16), s5_glu_b[layer][None, :],
                 w_proj_b[layer].astype(BF16), w_out[layer].astype(BF16), final_norm_w[None, :], seq)
    return out.reshape(bsz, seq, d)
```

```python
import functools
import math

import jax
import jax.numpy as jnp
from jax import lax
from jax.experimental import pallas as pl
from jax.experimental.pallas import tpu as pltpu

D_MODEL = 1024
DN_HEADS = 8
DN_HEAD_DIM = 128
DN_WIDTH = DN_HEADS * DN_HEAD_DIM
CONV_K = 4
CHUNK = 64
S5_WIDTH = D_MODEL // 2
S5_GROUP = 16
S5_GROUPS = S5_WIDTH // S5_GROUP
S5_STATE = 64
EPS = 1e-6
LANES = 128
VMEM_LIMIT = 56 * 1024 * 1024

S5_T = 8
S5_GPT = LANES // S5_GROUP
S5_NQ = S5_WIDTH // LANES
S5_SP = S5_GPT * S5_STATE
S5_TW = S5_T * LANES
S5_LB = 4096
S5_SUB = 8
S5_NCH = S5_LB // S5_T
S5_NVR = S5_NCH // S5_SUB
S5_SEG = S5_LB // S5_SUB
S5_MULROWS = 72

BF16 = jnp.bfloat16
F32 = jnp.float32


def _cparams(sem):
    return pltpu.CompilerParams(dimension_semantics=sem, vmem_limit_bytes=VMEM_LIMIT)


def _sigmoid(x):
    return 1.0 / (1.0 + jnp.exp(-x))


def _silu(x):
    return x * _sigmoid(x)


def _dot(a, b):
    return jnp.dot(a.astype(BF16), b.astype(BF16), preferred_element_type=F32)


def _dot_nt(a, b):
    return lax.dot_general(a.astype(BF16), b.astype(BF16), (((1,), (1,)), ((), ())),
                           preferred_element_type=F32)


def _dot_tn(a, b):
    return lax.dot_general(a.astype(BF16), b.astype(BF16), (((0,), (0,)), ((), ())),
                           preferred_element_type=F32)


def _ada_kernel(c_ref, w_ref, b_ref, o_ref):
    o_ref[...] = jnp.dot(c_ref[...], w_ref[...], preferred_element_type=F32,
                         precision=lax.Precision.HIGHEST) + b_ref[...]


def _ada(c8, w_ada, b_ada):
    n3 = w_ada.shape[1]
    tn = 1024
    return pl.pallas_call(
        _ada_kernel,
        grid=(n3 // tn,),
        in_specs=[pl.BlockSpec((8, D_MODEL), lambda j: (0, 0)),
                  pl.BlockSpec((D_MODEL, tn), lambda j: (0, j)),
                  pl.BlockSpec((1, tn), lambda j: (0, j))],
        out_specs=pl.BlockSpec((8, tn), lambda j: (0, j)),
        out_shape=jax.ShapeDtypeStruct((8, n3), F32),
        compiler_params=_cparams(("arbitrary",)),
        name="ada",
    )(c8, w_ada, b_ada)


INPROJ_TM = 256
INPROJ_TN = 512
PAD = 8
COL_ZA = 3 * DN_WIDTH
COL_U = COL_ZA + DN_WIDTH
COL_ZB = COL_U + S5_WIDTH
COL_GA = COL_ZB + S5_WIDTH
COL_GB = COL_GA + D_MODEL
COL_BA = COL_GB + D_MODEL


def _inproj_kernel(x_ref, mod_ref, w_ref, cw_ref,
                   q_ref, k_ref, v_ref, za_ref, ba_ref, u_ref, zb_ref, ga_ref, gb_ref, pad_s, *, tiles_per_seq):
    tm = INPROJ_TM
    x = x_ref[...]
    xn = x * lax.rsqrt(jnp.mean(x * x, axis=-1, keepdims=True) + EPS)
    shift = mod_ref[0, :, 0:D_MODEL]
    scale = mod_ref[0, :, D_MODEL:2 * D_MODEL]
    h = (xn * (1.0 + scale) + shift).astype(BF16)

    @pl.when(pl.program_id(0) % tiles_per_seq == 0)
    def _():
        pad_s[0:PAD, :] = jnp.zeros((PAD, 3 * DN_WIDTH), F32)

    def proj(lo, width):
        return jnp.dot(h, w_ref[:, lo:lo + width], preferred_element_type=F32)

    heads_per_chunk = INPROJ_TN // DN_HEAD_DIM
    for j in range(3 * DN_WIDTH // INPROJ_TN):
        lo = j * INPROJ_TN
        cs = slice(lo, lo + INPROJ_TN)
        pad_s[PAD:PAD + tm, cs] = proj(lo, INPROJ_TN)
        acc = pad_s[PAD:PAD + tm, cs] * cw_ref[CONV_K - 1:CONV_K, cs]
        for t in range(CONV_K - 1):
            sh = CONV_K - 1 - t
            acc = acc + pad_s[PAD - sh:PAD - sh + tm, cs] * cw_ref[t:t + 1, cs]
        y = _silu(acc)
        pad_s[0:PAD, cs] = pad_s[tm:tm + PAD, cs]
        which = lo // DN_WIDTH
        out_ref = (q_ref, k_ref, v_ref)[which]
        olo = lo - which * DN_WIDTH
        if which == 2:
            out_ref[:, olo:olo + INPROJ_TN] = y.astype(BF16)
        else:
            post = DN_HEAD_DIM ** -0.5 if which == 0 else 1.0
            for hh in range(heads_per_chunk):
                yh = y[:, hh * DN_HEAD_DIM:(hh + 1) * DN_HEAD_DIM]
                yh = yh * (lax.rsqrt(jnp.sum(yh * yh, axis=-1, keepdims=True) + EPS) * post)
                out_ref[:, olo + hh * DN_HEAD_DIM:olo + (hh + 1) * DN_HEAD_DIM] = yh.astype(BF16)

    def cols(lo, out_ref, width, act):
        for j in range(width // INPROJ_TN):
            out_ref[:, j * INPROJ_TN:(j + 1) * INPROJ_TN] = act(proj(lo + j * INPROJ_TN, INPROJ_TN)).astype(out_ref.dtype)

    cols(COL_ZA, za_ref, DN_WIDTH, _silu)
    pu = proj(COL_U, S5_WIDTH)
    for qt in range(S5_NQ):
        u_ref[qt] = pu[:, qt * LANES:(qt + 1) * LANES]
    cols(COL_ZB, zb_ref, S5_WIDTH, _silu)
    cols(COL_GA, ga_ref, D_MODEL, _sigmoid)
    cols(COL_GB, gb_ref, D_MODEL, _sigmoid)
    ba_ref[...] = proj(COL_BA, LANES)


def _inproj(x2, mod3, w, conv_w, seq):
    n = x2.shape[0]
    tm = INPROJ_TM
    per_b = seq // tm
    row = lambda i: (i, 0)
    const = lambda i: (0, 0)
    outs = [(DN_WIDTH, BF16)] * 3 + [(DN_WIDTH, BF16), (LANES, F32), None, (S5_WIDTH, BF16),
                                     (D_MODEL, BF16), (D_MODEL, BF16)]
    out_specs = [pl.BlockSpec((S5_NQ, tm, LANES), lambda i: (0, i, 0)) if o is None
                 else pl.BlockSpec((tm, o[0]), row) for o in outs]
    out_shape = [jax.ShapeDtypeStruct((S5_NQ, n, LANES), F32) if o is None
                 else jax.ShapeDtypeStruct((n, o[0]), o[1]) for o in outs]
    return pl.pallas_call(
        functools.partial(_inproj_kernel, tiles_per_seq=per_b),
        grid=(n // tm,),
        in_specs=[pl.BlockSpec((tm, D_MODEL), row),
                  pl.BlockSpec((1, 1, 3 * D_MODEL), lambda i: (i // per_b, 0, 0)),
                  pl.BlockSpec(w.shape, const),
                  pl.BlockSpec(conv_w.shape, const)],
        out_specs=out_specs,
        out_shape=out_shape,
        scratch_shapes=[pltpu.VMEM((tm + PAD, 3 * DN_WIDTH), F32)],
        compiler_params=_cparams(("arbitrary",)),
        name="inproj",
    )(x2, mod3, w, conv_w)


DELTA_LBLK = 256
DELTA_NC = DELTA_LBLK // CHUNK
DELTA_NI = DN_HEADS * DELTA_NC
DELTA_R = DELTA_NI * CHUNK


def _delta_kernel(q_ref, k_ref, v_ref, za_ref, ba_ref, gpar_ref, nw_ref, o_ref,
                  q_s, k_s, v_s, st_s, gcol_s, beta_s, glast_s, grow_s,
                  qk_s, am_s, t_s, x_s, wu_s, mq_s, n_s, o0_s, oo_s):
    lblk, nc, dh = DELTA_LBLK, DELTA_NC, DN_HEAD_DIM
    insts = [(h, c) for h in range(DN_HEADS) for c in range(nc)]

    def rows(h, c):
        r0 = (h * nc + c) * CHUNK
        return slice(r0, r0 + CHUNK)

    @pl.when(pl.program_id(1) == 0)
    def _():
        st_s[...] = jnp.zeros_like(st_s)

    for h in range(DN_HEADS):
        hs = slice(h * lblk, (h + 1) * lblk)
        q_s[hs, :] = q_ref[0, :, h * dh:(h + 1) * dh].astype(F32)
        k_s[hs, :] = k_ref[0, :, h * dh:(h + 1) * dh].astype(F32)
        v_s[hs, :] = v_ref[0, :, h * dh:(h + 1) * dh].astype(F32)

    ba = ba_ref[0]
    beta = _sigmoid(ba)
    zg = ba + gpar_ref[1:2, :]
    softplus = jnp.maximum(zg, 0.0) + jnp.log(1.0 + jnp.exp(-jnp.abs(zg)))
    g = -jnp.exp(gpar_ref[0:1, :]) * softplus
    rowi = lax.broadcasted_iota(jnp.int32, (lblk, LANES), 0) & (CHUNK - 1)
    d = 1
    while d < CHUNK:
        g = g + jnp.where(rowi >= d, pltpu.roll(g, d, axis=0), 0.0)
        d *= 2
    for h in range(DN_HEADS):
        gcol_s[h * lblk:(h + 1) * lblk, :] = jnp.broadcast_to(g[:, DN_HEADS + h:DN_HEADS + h + 1], (lblk, dh))
        beta_s[h * lblk:(h + 1) * lblk, :] = jnp.broadcast_to(beta[:, h:h + 1], (lblk, dh))
    for c in range(nc):
        gct = jnp.concatenate([g[c * CHUNK:(c + 1) * CHUNK, :], jnp.zeros((LANES - CHUNK, LANES), F32)], axis=0).T
        for h in range(DN_HEADS):
            grow_s[rows(h, c), :] = jnp.broadcast_to(gct[DN_HEADS + h:DN_HEADS + h + 1, 0:CHUNK], (CHUNK, CHUNK))
            glast_s[rows(h, c), :] = jnp.broadcast_to(
                g[(c + 1) * CHUNK - 1:(c + 1) * CHUNK, DN_HEADS + h:DN_HEADS + h + 1], (CHUNK, dh))

    for h, c in insts:
        r = rows(h, c)
        kq = _dot_nt(jnp.concatenate([q_s[r, :], k_s[r, :]], axis=0), k_s[r, :])
        qk_s[r, :] = kq[0:CHUNK]
        am_s[r, :] = kq[CHUNK:2 * CHUNK]

    ri = lax.broadcasted_iota(jnp.int32, (DELTA_R, CHUNK), 0) & (CHUNK - 1)
    ci = lax.broadcasted_iota(jnp.int32, (DELTA_R, CHUNK), 1)
    dec = jnp.where(ri >= ci, jnp.exp(jnp.minimum(gcol_s[:, 0:CHUNK] - grow_s[...], 0.0)), 0.0)
    qk_s[...] = qk_s[...] * dec
    amat = jnp.where(ri > ci, beta_s[:, 0:CHUNK] * am_s[...] * dec, 0.0)
    am_s[...] = amat
    t_s[...] = jnp.where(ri == ci, 1.0, 0.0) - jnp.where((ri >> 1) == (ci >> 1), amat, 0.0)
    r1 = lax.broadcasted_iota(jnp.int32, (CHUNK, CHUNK), 0)
    c1 = lax.broadcasted_iota(jnp.int32, (CHUNK, CHUNK), 1)
    lvl = 1
    while (1 << lvl) < CHUNK:
        off_mask = ((r1 >> (lvl + 1)) == (c1 >> (lvl + 1))) & ((r1 >> lvl) != (c1 >> lvl))
        for h, c in insts:
            r = rows(h, c)
            x_s[r, :] = _dot(jnp.where(off_mask, am_s[r, :], 0.0), t_s[r, :])
        for h, c in insts:
            r = rows(h, c)
            t_s[r, :] = t_s[r, :] - _dot(t_s[r, :], x_s[r, :])
        lvl += 1

    eg = jnp.exp(gcol_s[...])
    wu_s[:, 0:dh] = k_s[...] * (beta_s[...] * eg)
    wu_s[:, dh:2 * dh] = v_s[...] * beta_s[...]
    for h, c in insts:
        r = rows(h, c)
        wu_s[r, :] = _dot(t_s[r, :], wu_s[r, :])

    q_s[...] = q_s[...] * eg
    k_s[...] = k_s[...] * jnp.exp(glast_s[...] - gcol_s[...])
    for i, (h, c) in enumerate(insts):
        r = rows(h, c)
        wu = wu_s[r, :]
        aw = _dot(qk_s[r, :], wu)
        mn = _dot_tn(k_s[r, :], wu)
        mq_s[i, 0:dh, :] = (-mn[:, 0:dh]).astype(BF16)
        mq_s[i, dh:dh + CHUNK, :] = (q_s[r, :] - aw[:, 0:dh]).astype(BF16)
        n_s[i] = mn[:, dh:2 * dh]
        o0_s[r, :] = aw[:, dh:2 * dh]

    for c in range(nc):
        for h in range(DN_HEADS):
            i = h * nc + c
            r = rows(h, c)
            s = st_s[h]
            res = jnp.dot(mq_s[i], s.astype(BF16), preferred_element_type=F32)
            gl = jnp.exp(glast_s[r.start:r.start + 1, :])
            st_s[h] = s * gl + res[0:dh] + n_s[i]
            oo_s[r, :] = res[dh:dh + CHUNK] + o0_s[r, :]

    o = oo_s[...]
    on = o * lax.rsqrt(jnp.mean(o * o, axis=-1, keepdims=True) + EPS) * nw_ref[...]
    for h in range(DN_HEADS):
        o_ref[0, :, h * dh:(h + 1) * dh] = (on[h * lblk:(h + 1) * lblk, :]
                                            * za_ref[0, :, h * dh:(h + 1) * dh].astype(F32)).astype(BF16)


def _delta(q, k, v, za, ba, gpar, norm_w):
    b, seq, _ = za.shape
    lblk = DELTA_LBLK
    blk = pl.BlockSpec((1, lblk, DN_WIDTH), lambda bi, li: (bi, li, 0))
    wide = pltpu.VMEM((DELTA_R, DN_HEAD_DIM), F32)
    narrow = pltpu.VMEM((DELTA_R, CHUNK), F32)
    return pl.pallas_call(
        _delta_kernel,
        grid=(b, seq // lblk),
        in_specs=[blk, blk, blk, blk,
                  pl.BlockSpec((1, lblk, LANES), lambda bi, li: (bi, li, 0)),
                  pl.BlockSpec((8, LANES), lambda bi, li: (0, 0)),
                  pl.BlockSpec((1, DN_HEAD_DIM), lambda bi, li: (0, 0))],
        out_specs=blk,
        out_shape=jax.ShapeDtypeStruct((b, seq, DN_WIDTH), BF16),
        scratch_shapes=[wide] * 3
        + [pltpu.VMEM((DN_HEADS, DN_HEAD_DIM, DN_HEAD_DIM), F32)]
        + [wide] * 3
        + [narrow] * 5
        + [pltpu.VMEM((DELTA_R, 2 * DN_HEAD_DIM), F32),
           pltpu.VMEM((DELTA_NI, DN_HEAD_DIM + CHUNK, DN_HEAD_DIM), BF16),
           pltpu.VMEM((DELTA_NI, DN_HEAD_DIM, DN_HEAD_DIM), F32),
           wide, wide],
        compiler_params=_cparams(("arbitrary", "arbitrary")),
        name="delta",
    )(q, k, v, za, ba, gpar, norm_w)


def _s5_prep_kernel(bdt_ref, bd_ref, cd_ref, lrow_ref, w_ref, pint_ref, pout_ref, mul_ref, col_s):
    hi = lax.Precision.HIGHEST
    dt = jnp.exp(lrow_ref[0, 2:3, :])
    lr = jnp.minimum(lrow_ref[0, 0:1, :], -1e-4)
    li = lrow_ref[0, 1:2, :]

    def power(n):
        mag = jnp.exp(lr * dt * n)
        return mag * jnp.cos(li * dt * n), mag * jnp.sin(li * dt * n)

    ab_re, ab_im = power(1.0)
    den = lr * lr + li * li
    f_re = ((ab_re - 1.0) * lr + ab_im * li) / den
    f_im = (ab_im * lr - (ab_re - 1.0) * li) / den

    bt_re, bt_im = bdt_ref[0, 0], bdt_ref[0, 1]
    bbt_re = bt_re * f_re - bt_im * f_im
    bbt_im = bt_im * f_re + bt_re * f_im
    c_re, c_im = cd_ref[0, 0], cd_ref[0, 1]
    w_ref[...] = jnp.zeros_like(w_ref)
    col_s[...] = jnp.zeros_like(col_s)
    for tau in range(S5_T + 1):
        p_re, p_im = power(float(tau))
        col_s[tau:tau + 1, :] = p_re
        col_s[16 + tau:17 + tau, :] = p_im
        if tau == S5_T:
            break
        l_re = bbt_re * p_re - bbt_im * p_im
        l_im = bbt_re * p_im + bbt_im * p_re
        tile = (jnp.dot(l_re, c_re, preferred_element_type=F32, precision=hi)
                - jnp.dot(l_im, c_im, preferred_element_type=F32, precision=hi)).astype(BF16)
        for tl in range(S5_T - tau):
            j = tl + tau
            w_ref[0, tl * LANES:(tl + 1) * LANES, j * LANES:(j + 1) * LANES] = tile
    col_s[32:33, :] = f_re
    col_s[33:34, :] = f_im
    nn = lax.broadcasted_iota(jnp.int32, (S5_MULROWS, S5_SP), 0).astype(F32) * float(S5_T)
    m_re, m_im = power(nn)
    mul_ref[0, 0] = m_re
    mul_ref[0, 1] = m_im
    colt = col_s[...].T
    bc = lambda k: jnp.broadcast_to(colt[:, k:k + 1], (S5_SP, LANES))
    cf_re, cf_im = bc(32), bc(33)
    b_re, b_im = bd_ref[0, 0], bd_ref[0, 1]
    bb_re = b_re * cf_re - b_im * cf_im
    bb_im = b_im * cf_re + b_re * cf_im
    for tl in range(S5_T):
        p_re, p_im = bc(S5_T - 1 - tl), bc(16 + S5_T - 1 - tl)
        pint_ref[0, 0:S5_SP, tl * LANES:(tl + 1) * LANES] = (bb_re * p_re - bb_im * p_im).astype(BF16)
        pint_ref[0, S5_SP:2 * S5_SP, tl * LANES:(tl + 1) * LANES] = (bb_re * p_im + bb_im * p_re).astype(BF16)
    for j in range(S5_T):
        p_re, p_im = bc(j + 1), bc(16 + j + 1)
        pout_ref[0, 0:S5_SP, j * LANES:(j + 1) * LANES] = (c_re * p_re - c_im * p_im).astype(BF16)
        pout_ref[0, S5_SP:2 * S5_SP, j * LANES:(j + 1) * LANES] = (-(c_re * p_im + c_im * p_re)).astype(BF16)


def _s5_prep(bdt, bd, cd, lrow):
    per = lambda a: pl.BlockSpec((1,) + a.shape[1:], lambda qi: (qi,) + (0,) * (a.ndim - 1))
    outs = [jax.ShapeDtypeStruct((S5_NQ, S5_TW, S5_TW), BF16),
            jax.ShapeDtypeStruct((S5_NQ, 2 * S5_SP, S5_TW), BF16),
            jax.ShapeDtypeStruct((S5_NQ, 2 * S5_SP, S5_TW), BF16),
            jax.ShapeDtypeStruct((S5_NQ, 2, S5_MULROWS, S5_SP), F32)]
    return pl.pallas_call(
        _s5_prep_kernel,
        grid=(S5_NQ,),
        in_specs=[per(bdt), per(bd), per(cd), per(lrow)],
        out_specs=[per(o) for o in outs],
        out_shape=outs,
        scratch_shapes=[pltpu.VMEM((LANES, S5_SP), F32)],
        compiler_params=_cparams(("arbitrary",)),
        name="s5prep",
    )(bdt, bd, cd, lrow)


def _s5_kernel(u_ref, w_ref, pint_ref, pout_ref, mul_ref, dsk_ref, y_ref, sre_s, sim_s, carry_s, seg_s):
    @pl.when(pl.program_id(2) == 0)
    def _reset():
        carry_s[...] = jnp.zeros_like(carry_s)

    xs = [[u_ref[0, 0, pl.ds(i * S5_T + tl, S5_SUB, stride=S5_SEG), :] for i in range(S5_NVR)]
          for tl in range(S5_T)]
    uc = jnp.concatenate([jnp.concatenate(xs[tl], axis=0).astype(BF16) for tl in range(S5_T)], axis=1)
    sin = lax.dot_general(uc, pint_ref[0], (((1,), (1,)), ((), ())), preferred_element_type=F32)
    sre_s[...] = sin[:, 0:S5_SP]
    sim_s[...] = sin[:, S5_SP:2 * S5_SP]

    l_re = mul_ref[0, 0, 1:2, :]
    l_im = mul_ref[0, 1, 1:2, :]

    def scan_step(i, carry):
        pr, pi = carry
        r0 = pl.multiple_of(i * S5_SUB, S5_SUB)
        xr = sre_s[pl.ds(r0, S5_SUB), :] + l_re * pr - l_im * pi
        xi = sim_s[pl.ds(r0, S5_SUB), :] + l_re * pi + l_im * pr
        sre_s[pl.ds(r0, S5_SUB), :] = xr
        sim_s[pl.ds(r0, S5_SUB), :] = xi
        return xr, xi

    zero = jnp.zeros((S5_SUB, S5_SP), F32)
    e_re, e_im = lax.fori_loop(0, S5_NVR, scan_step, (zero, zero))

    g_re = mul_ref[0, 0, S5_NVR:S5_NVR + 1, :]
    g_im = mul_ref[0, 1, S5_NVR:S5_NVR + 1, :]
    seg_s[0] = e_re
    seg_s[1] = e_im
    c_re = carry_s[0:1, :]
    c_im = carry_s[1:2, :]
    for sgi in range(S5_SUB):
        seg_s[2, sgi:sgi + 1, :] = c_re
        seg_s[3, sgi:sgi + 1, :] = c_im
        n_re = g_re * c_re - g_im * c_im + seg_s[0, sgi:sgi + 1, :]
        n_im = g_re * c_im + g_im * c_re + seg_s[1, sgi:sgi + 1, :]
        c_re, c_im = n_re, n_im
    carry_s[0:1, :] = c_re
    carry_s[1:2, :] = c_im

    cs_re = jnp.broadcast_to(seg_s[2][None], (S5_NVR, S5_SUB, S5_SP)).reshape(S5_NCH, S5_SP)
    cs_im = jnp.broadcast_to(seg_s[3][None], (S5_NVR, S5_SUB, S5_SP)).reshape(S5_NCH, S5_SP)
    rep = lambda a: jnp.broadcast_to(a[:, None, :], (S5_NVR, S5_SUB, S5_SP)).reshape(S5_NCH, S5_SP)
    m_re = rep(mul_ref[0, 0, 0:S5_NVR, :])
    m_im = rep(mul_ref[0, 1, 0:S5_NVR, :])
    zrow = jnp.zeros((S5_SUB, S5_SP), F32)
    sh_re = jnp.concatenate([zrow, sre_s[0:S5_NCH - S5_SUB, :]], axis=0)
    sh_im = jnp.concatenate([zrow, sim_s[0:S5_NCH - S5_SUB, :]], axis=0)
    pv_re = sh_re + m_re * cs_re - m_im * cs_im
    pv_im = sh_im + m_re * cs_im + m_im * cs_re
    xprev = jnp.concatenate([pv_re.astype(BF16), pv_im.astype(BF16)], axis=1)
    y = (jnp.dot(uc, w_ref[0], preferred_element_type=F32)
         + jnp.dot(xprev, pout_ref[0], preferred_element_type=F32))
    dq = dsk_ref[0]
    for j in range(S5_T):
        for i in range(S5_NVR):
            y_ref[0, 0, pl.ds(i * S5_T + j, S5_SUB, stride=S5_SEG), :] = (
                y[i * S5_SUB:(i + 1) * S5_SUB, j * LANES:(j + 1) * LANES] + xs[j][i] * dq)


def _s5_params(lam_re, lam_im, log_dt, b_re, b_im, c_re, c_im, d_skip):
    same = jnp.eye(S5_GPT, dtype=bool)[None, :, None, :, None]

    def bdiag(a):
        a = a.reshape(S5_NQ, S5_GPT, S5_STATE, 1, S5_GROUP)
        return jnp.where(same, a, 0.0).reshape(S5_NQ, S5_SP, LANES)

    bd = jnp.stack([bdiag(b_re), bdiag(b_im)], axis=1)
    bdt = jnp.swapaxes(bd, 2, 3)
    cd = jnp.stack([bdiag(jnp.swapaxes(c_re, 1, 2)), bdiag(jnp.swapaxes(c_im, 1, 2))], axis=1)
    dtb = jnp.broadcast_to(log_dt[:, None], lam_re.shape)
    lrow = jnp.stack([lam_re.reshape(S5_NQ, S5_SP), lam_im.reshape(S5_NQ, S5_SP), dtb.reshape(S5_NQ, S5_SP)], axis=1)
    lrow = jnp.pad(lrow, ((0, 0), (0, 5), (0, 0)))
    dsk = d_skip.reshape(S5_NQ, 1, LANES)
    return bdt, bd, cd, lrow, dsk


def _s5(u4, bdt, bd, cd, lrow, dsk):
    _, b, seq, _ = u4.shape
    w, pint, pout, mul = _s5_prep(bdt, bd, cd, lrow)
    perq = lambda a: pl.BlockSpec((1,) + a.shape[1:], lambda qi, bi, li: (qi,) + (0,) * (a.ndim - 1))
    blk = pl.BlockSpec((1, 1, S5_LB, LANES), lambda qi, bi, li: (qi, bi, li, 0))
    return pl.pallas_call(
        _s5_kernel,
        grid=(S5_NQ, b, seq // S5_LB),
        in_specs=[blk, perq(w), perq(pint), perq(pout), perq(mul), perq(dsk)],
        out_specs=blk,
        out_shape=jax.ShapeDtypeStruct((S5_NQ, b, seq, LANES), F32),
        scratch_shapes=[pltpu.VMEM((S5_NCH, S5_SP), F32),
                        pltpu.VMEM((S5_NCH, S5_SP), F32),
                        pltpu.VMEM((8, S5_SP), F32),
                        pltpu.VMEM((4, S5_SUB, S5_SP), F32)],
        compiler_params=_cparams(("arbitrary", "arbitrary", "arbitrary")),
        name="s5",
    )(u4, w, pint, pout, mul, dsk)


FINAL_TM = 256


def _final_kernel(x_ref, og_ref, ys_ref, zb_ref, ga_ref, gb_ref, mod_ref,
                  wpa_ref, glw_ref, glb_ref, wpb_ref, wo_ref, fnw_ref, out_ref):
    ya = jnp.dot(og_ref[...], wpa_ref[...], preferred_element_type=F32)
    ys = jnp.concatenate([ys_ref[qt] for qt in range(S5_NQ)], axis=1)
    gy = 0.5 * ys * (1.0 + jnp.tanh(math.sqrt(2.0 / math.pi) * (ys + 0.044715 * (ys * ys * ys))))
    glu = jnp.dot(gy.astype(BF16), glw_ref[...], preferred_element_type=F32) + glb_ref[...]
    y2 = gy * _sigmoid(glu)
    yb = jnp.dot((y2 * zb_ref[...].astype(F32)).astype(BF16), wpb_ref[...], preferred_element_type=F32)
    merged = ga_ref[...].astype(F32) * ya + gb_ref[...].astype(F32) * yb
    mo = jnp.dot(merged.astype(BF16), wo_ref[...], preferred_element_type=F32)
    gate = mod_ref[0, :, 2 * D_MODEL:3 * D_MODEL]
    xo = x_ref[...] + gate * mo
    out_ref[...] = xo * lax.rsqrt(jnp.mean(xo * xo, axis=-1, keepdims=True) + EPS) * fnw_ref[...]


def _final(x2, og, ys, zb, ga, gb, mod3, wpa, glw, glb, wpb, wo, fnw, seq):
    n = x2.shape[0]
    tm = FINAL_TM
    per_b = seq // tm
    row = lambda w: pl.BlockSpec((tm, w), lambda i: (i, 0))
    const = lambda a: pl.BlockSpec(a.shape, lambda i: (0,) * a.ndim)
    return pl.pallas_call(
        _final_kernel,
        grid=(n // tm,),
        in_specs=[row(D_MODEL), row(DN_WIDTH), pl.BlockSpec((S5_NQ, tm, LANES), lambda i: (0, i, 0)),
                  row(S5_WIDTH), row(D_MODEL), row(D_MODEL),
                  pl.BlockSpec((1, 1, 3 * D_MODEL), lambda i: (i // per_b, 0, 0)),
                  const(wpa), const(glw), const(glb), const(wpb), const(wo), const(fnw)],
        out_specs=row(D_MODEL),
        out_shape=jax.ShapeDtypeStruct((n, D_MODEL), F32),
        compiler_params=_cparams(("arbitrary",)),
        name="final",
    )(x2, og, ys, zb, ga, gb, mod3, wpa, glw, glb, wpb, wo, fnw)


def kernel(x, c, w_ada, b_ada, w_in, dn_conv_w, dn_a_log, dn_dt_bias, dn_norm_w, w_proj_a,
           s5_lambda_re, s5_lambda_im, s5_log_dt, s5_b_re, s5_b_im, s5_c_re, s5_c_im, s5_d,
           s5_glu_w, s5_glu_b, w_proj_b, w_out, final_norm_w):
    bsz, seq, d = x.shape
    n = bsz * seq
    assert w_ada.shape[0] == 1, "the final rmsnorm is fused into the single layer's epilogue"
    layer = 0
    x2 = x.reshape(n, d)
    c8 = jnp.zeros((8, d), F32).at[:bsz].set(c)
    mod = _ada(c8, w_ada[layer], b_ada[layer][None, :])[:bsz]
    mod3 = mod.reshape(bsz, 1, 3 * d)

    wl = w_in[layer]
    o_ba = 4 * DN_WIDTH
    o_rest = o_ba + 2 * DN_HEADS
    w = jnp.concatenate([wl[:, :o_ba], wl[:, o_rest:], wl[:, o_ba:o_rest],
                         jnp.zeros((d, LANES - 2 * DN_HEADS), wl.dtype)], axis=1).astype(BF16)
    q, k, v, za, ba, u, zb, ga, gb = _inproj(x2, mod3, w, dn_conv_w[layer], seq)

    gpar = jnp.zeros((8, LANES), F32)
    gpar = gpar.at[0, DN_HEADS:2 * DN_HEADS].set(dn_a_log[layer])
    gpar = gpar.at[1, DN_HEADS:2 * DN_HEADS].set(dn_dt_bias[layer])
    r3 = lambda a: a.reshape(bsz, seq, a.shape[-1])
    og = _delta(r3(q), r3(k), r3(v), r3(za), r3(ba), gpar, dn_norm_w[layer][None, :])

    ys = _s5(u.reshape(S5_NQ, bsz, seq, LANES),
             *_s5_params(s5_lambda_re[layer], s5_lambda_im[layer], s5_log_dt[layer], s5_b_re[layer],
                         s5_b_im[layer], s5_c_re[layer], s5_c_im[layer], s5_d[layer])).reshape(S5_NQ, n, LANES)

    out = _final(x2, og.reshape(n, DN_WIDTH), ys, zb, ga, gb, mod3,
                 w_proj_a[layer].astype(BF16), s5_glu_w[layer].astype(BF16), s5_glu_b[layer][None, :],
                 w_proj_b[layer].astype(BF16), w_out[layer].astype(BF16), final_norm_w[None, :], seq)
    return out.reshape(bsz, seq, d)
```

```python
import functools
import math

import jax
import jax.numpy as jnp
from jax import lax
from jax.experimental import pallas as pl
from jax.experimental.pallas import tpu as pltpu

D_MODEL = 1024
DN_HEADS = 8
DN_HEAD_DIM = 128
DN_WIDTH = DN_HEADS * DN_HEAD_DIM
CONV_K = 4
CHUNK = 64
S5_WIDTH = D_MODEL // 2
S5_GROUP = 16
S5_GROUPS = S5_WIDTH // S5_GROUP
S5_STATE = 64
EPS = 1e-6
LANES = 128
VMEM_LIMIT = 56 * 1024 * 1024

S5_T = 8
S5_GPT = LANES // S5_GROUP
S5_NQ = S5_WIDTH // LANES
S5_SP = S5_GPT * S5_STATE
S5_TW = S5_T * LANES
S5_LB = 4096
S5_SUB = 8
S5_NCH = S5_LB // S5_T
S5_NVR = S5_NCH // S5_SUB
S5_SEG = S5_LB // S5_SUB
S5_MULROWS = 72

BF16 = jnp.bfloat16
F32 = jnp.float32


def _cparams(sem):
    return pltpu.CompilerParams(dimension_semantics=sem, vmem_limit_bytes=VMEM_LIMIT)


def _sigmoid(x):
    return 1.0 / (1.0 + jnp.exp(-x))


def _silu(x):
    return x * _sigmoid(x)


def _dot(a, b):
    return jnp.dot(a.astype(BF16), b.astype(BF16), preferred_element_type=F32)


def _dot_nt(a, b):
    return lax.dot_general(a.astype(BF16), b.astype(BF16), (((1,), (1,)), ((), ())),
                           preferred_element_type=F32)


def _dot_tn(a, b):
    return lax.dot_general(a.astype(BF16), b.astype(BF16), (((0,), (0,)), ((), ())),
                           preferred_element_type=F32)


def _ada_kernel(c_ref, w_ref, b_ref, o_ref):
    o_ref[...] = jnp.dot(c_ref[...], w_ref[...], preferred_element_type=F32,
                         precision=lax.Precision.HIGHEST) + b_ref[...]


def _ada(c8, w_ada, b_ada):
    n3 = w_ada.shape[1]
    tn = 1024
    return pl.pallas_call(
        _ada_kernel,
        grid=(n3 // tn,),
        in_specs=[pl.BlockSpec((8, D_MODEL), lambda j: (0, 0)),
                  pl.BlockSpec((D_MODEL, tn), lambda j: (0, j)),
                  pl.BlockSpec((1, tn), lambda j: (0, j))],
        out_specs=pl.BlockSpec((8, tn), lambda j: (0, j)),
        out_shape=jax.ShapeDtypeStruct((8, n3), F32),
        compiler_params=_cparams(("arbitrary",)),
        name="ada",
    )(c8, w_ada, b_ada)


INPROJ_TM = 256
INPROJ_TN = 512
PAD = 8
COL_ZA = 3 * DN_WIDTH
COL_U = COL_ZA + DN_WIDTH
COL_ZB = COL_U + S5_WIDTH
COL_GA = COL_ZB + S5_WIDTH
COL_GB = COL_GA + D_MODEL
COL_BA = COL_GB + D_MODEL


def _inproj_kernel(x_ref, mod_ref, w_ref, cw_ref,
                   q_ref, k_ref, v_ref, za_ref, ba_ref, u_ref, zb_ref, ga_ref, gb_ref, pad_s, *, tiles_per_seq):
    tm = INPROJ_TM
    x = x_ref[...]
    xn = x * lax.rsqrt(jnp.mean(x * x, axis=-1, keepdims=True) + EPS)
    shift = mod_ref[0, :, 0:D_MODEL]
    scale = mod_ref[0, :, D_MODEL:2 * D_MODEL]
    h = (xn * (1.0 + scale) + shift).astype(BF16)

    @pl.when(pl.program_id(0) % tiles_per_seq == 0)
    def _():
        pad_s[0:PAD, :] = jnp.zeros((PAD, 3 * DN_WIDTH), F32)

    def proj(lo, width):
        return jnp.dot(h, w_ref[:, lo:lo + width], preferred_element_type=F32)

    heads_per_chunk = INPROJ_TN // DN_HEAD_DIM
    for j in range(3 * DN_WIDTH // INPROJ_TN):
        lo = j * INPROJ_TN
        cs = slice(lo, lo + INPROJ_TN)
        pad_s[PAD:PAD + tm, cs] = proj(lo, INPROJ_TN)
        acc = pad_s[PAD:PAD + tm, cs] * cw_ref[CONV_K - 1:CONV_K, cs]
        for t in range(CONV_K - 1):
            sh = CONV_K - 1 - t
            acc = acc + pad_s[PAD - sh:PAD - sh + tm, cs] * cw_ref[t:t + 1, cs]
        y = _silu(acc)
        pad_s[0:PAD, cs] = pad_s[tm:tm + PAD, cs]
        which = lo // DN_WIDTH
        out_ref = (q_ref, k_ref, v_ref)[which]
        olo = lo - which * DN_WIDTH
        if which == 2:
            out_ref[:, olo:olo + INPROJ_TN] = y.astype(BF16)
        else:
            post = DN_HEAD_DIM ** -0.5 if which == 0 else 1.0
            for hh in range(heads_per_chunk):
                yh = y[:, hh * DN_HEAD_DIM:(hh + 1) * DN_HEAD_DIM]
                yh = yh * (lax.rsqrt(jnp.sum(yh * yh, axis=-1, keepdims=True) + EPS) * post)
                out_ref[:, olo + hh * DN_HEAD_DIM:olo + (hh + 1) * DN_HEAD_DIM] = yh.astype(BF16)

    def cols(lo, out_ref, width, act):
        for j in range(width // INPROJ_TN):
            out_ref[:, j * INPROJ_TN:(j + 1) * INPROJ_TN] = act(proj(lo + j * INPROJ_TN, INPROJ_TN)).astype(out_ref.dtype)

    cols(COL_ZA, za_ref, DN_WIDTH, _silu)
    pu = proj(COL_U, S5_WIDTH)
    for qt in range(S5_NQ):
        u_ref[qt] = pu[:, qt * LANES:(qt + 1) * LANES]
    cols(COL_ZB, zb_ref, S5_WIDTH, _silu)
    cols(COL_GA, ga_ref, D_MODEL, _sigmoid)
    cols(COL_GB, gb_ref, D_MODEL, _sigmoid)
    ba_ref[...] = proj(COL_BA, LANES)


def _inproj(x2, mod3, w, conv_w, seq):
    n = x2.shape[0]
    tm = INPROJ_TM
    per_b = seq // tm
    row = lambda i: (i, 0)
    const = lambda i: (0, 0)
    outs = [(DN_WIDTH, BF16)] * 3 + [(DN_WIDTH, BF16), (LANES, F32), None, (S5_WIDTH, BF16),
                                     (D_MODEL, BF16), (D_MODEL, BF16)]
    out_specs = [pl.BlockSpec((S5_NQ, tm, LANES), lambda i: (0, i, 0)) if o is None
                 else pl.BlockSpec((tm, o[0]), row) for o in outs]
    out_shape = [jax.ShapeDtypeStruct((S5_NQ, n, LANES), F32) if o is None
                 else jax.ShapeDtypeStruct((n, o[0]), o[1]) for o in outs]
    return pl.pallas_call(
        functools.partial(_inproj_kernel, tiles_per_seq=per_b),
        grid=(n // tm,),
        in_specs=[pl.BlockSpec((tm, D_MODEL), row),
                  pl.BlockSpec((1, 1, 3 * D_MODEL), lambda i: (i // per_b, 0, 0)),
                  pl.BlockSpec(w.shape, const),
                  pl.BlockSpec(conv_w.shape, const)],
        out_specs=out_specs,
        out_shape=out_shape,
        scratch_shapes=[pltpu.VMEM((tm + PAD, 3 * DN_WIDTH), F32)],
        compiler_params=_cparams(("arbitrary",)),
        name="inproj",
    )(x2, mod3, w, conv_w)


DELTA_LBLK = 256
DELTA_NC = DELTA_LBLK // CHUNK
DELTA_NI = DN_HEADS * DELTA_NC
DELTA_R = DELTA_NI * CHUNK


def _blockdiag(x_cat, mask):
    xb = x_cat.astype(BF16)
    return jnp.where(mask, jnp.concatenate([xb] * DELTA_NC, axis=0), jnp.zeros((), BF16))


def _delta_kernel(q_ref, k_ref, v_ref, za_ref, ba_ref, gpar_ref, nw_ref, o_ref,
                  q_s, k_s, v_s, st_s, gcol_s, beta_s, glast_s,
                  qk_s, am_s, t_s, x_s, wu_s, mq_s, n_s, o0_s, oo_s):
    lblk, nc, dh = DELTA_LBLK, DELTA_NC, DN_HEAD_DIM
    cw = nc * CHUNK
    heads = range(DN_HEADS)

    def rows(h, c):
        r0 = (h * nc + c) * CHUNK
        return slice(r0, r0 + CHUNK)

    hrows = lambda h: slice(h * lblk, (h + 1) * lblk)
    crows = lambda h: slice(h * CHUNK, (h + 1) * CHUNK)

    @pl.when(pl.program_id(1) == 0)
    def _():
        st_s[...] = jnp.zeros_like(st_s)

    for h in heads:
        q_s[hrows(h), :] = q_ref[0, :, h * dh:(h + 1) * dh].astype(F32)
        k_s[hrows(h), :] = k_ref[0, :, h * dh:(h + 1) * dh].astype(F32)
        v_s[hrows(h), :] = v_ref[0, :, h * dh:(h + 1) * dh].astype(F32)

    ba = ba_ref[0]
    beta = _sigmoid(ba)
    zg = ba + gpar_ref[1:2, :]
    softplus = jnp.maximum(zg, 0.0) + jnp.log(1.0 + jnp.exp(-jnp.abs(zg)))
    g = -jnp.exp(gpar_ref[0:1, :]) * softplus
    rowi = lax.broadcasted_iota(jnp.int32, (lblk, LANES), 0) & (CHUNK - 1)
    d = 1
    while d < CHUNK:
        g = g + jnp.where(rowi >= d, pltpu.roll(g, d, axis=0), 0.0)
        d *= 2
    gt = g.T
    for h in heads:
        gcol_s[hrows(h), :] = jnp.broadcast_to(g[:, DN_HEADS + h:DN_HEADS + h + 1], (lblk, dh))
        beta_s[hrows(h), :] = jnp.broadcast_to(beta[:, h:h + 1], (lblk, dh))
        for c in range(nc):
            glast_s[rows(h, c), :] = jnp.broadcast_to(
                g[(c + 1) * CHUNK - 1:(c + 1) * CHUNK, DN_HEADS + h:DN_HEADS + h + 1], (CHUNK, dh))

    def to_cat(ref, h):
        lane = lax.broadcasted_iota(jnp.int32, (CHUNK, LANES), 1)
        pieces = [ref[rows(h, c), :] for c in range(nc)]
        return jnp.concatenate([jnp.where(lane < CHUNK, pieces[2 * m], pieces[2 * m + 1]) for m in range(nc // 2)],
                               axis=1)

    rk = lax.broadcasted_iota(jnp.int32, (lblk, nc * dh), 0) // CHUNK
    ck = lax.broadcasted_iota(jnp.int32, (lblk, nc * dh), 1) // dh
    kmask = rk == ck
    for h in heads:
        qcat = jnp.concatenate([q_s[rows(h, c), :] for c in range(nc)], axis=1)
        kcat = jnp.concatenate([k_s[rows(h, c), :] for c in range(nc)], axis=1)
        kb = k_s[hrows(h), :].astype(BF16)
        kbd = jnp.where(kmask, jnp.concatenate([kb] * nc, axis=1), jnp.zeros((), BF16))
        kq = _dot_nt(jnp.concatenate([qcat, kcat], axis=0), kbd)
        qk_s[crows(h), :] = kq[0:CHUNK]
        am_s[crows(h), :] = kq[CHUNK:2 * CHUNK]

    nr = DN_HEADS * CHUNK
    ri = lax.broadcasted_iota(jnp.int32, (nr, cw), 0) & (CHUNK - 1)
    ci = lax.broadcasted_iota(jnp.int32, (nr, cw), 1) & (CHUNK - 1)
    gcolc = jnp.concatenate([to_cat(gcol_s, h) for h in heads], axis=0)
    betac = jnp.concatenate([to_cat(beta_s, h) for h in heads], axis=0)
    growc = jnp.concatenate([jnp.broadcast_to(gt[DN_HEADS + h:DN_HEADS + h + 1, :], (CHUNK, cw)) for h in heads],
                            axis=0)
    dec = jnp.where(ri >= ci, jnp.exp(jnp.minimum(gcolc - growc, 0.0)), 0.0)
    qk_s[...] = qk_s[...] * dec
    amat = jnp.where(ri > ci, betac * am_s[...] * dec, 0.0)
    am_s[...] = amat
    t_s[...] = jnp.where(ri == ci, 1.0, 0.0) - jnp.where((ri >> 1) == (ci >> 1), amat, 0.0)
    r1 = lax.broadcasted_iota(jnp.int32, (CHUNK, cw), 0)
    c1 = lax.broadcasted_iota(jnp.int32, (CHUNK, cw), 1) & (CHUNK - 1)
    bdmask = (lax.broadcasted_iota(jnp.int32, (cw, cw), 0) // CHUNK) == (lax.broadcasted_iota(jnp.int32, (cw, cw), 1) // CHUNK)
    lvl = 1
    while (1 << lvl) < CHUNK:
        off_mask = ((r1 >> (lvl + 1)) == (c1 >> (lvl + 1))) & ((r1 >> lvl) != (c1 >> lvl))
        for h in heads:
            off = jnp.where(off_mask, am_s[crows(h), :], 0.0)
            x_s[crows(h), :] = jnp.dot(off.astype(BF16), _blockdiag(t_s[crows(h), :], bdmask), preferred_element_type=F32)
        for h in heads:
            t = t_s[crows(h), :]
            t_s[crows(h), :] = t - jnp.dot(t.astype(BF16), _blockdiag(x_s[crows(h), :], bdmask), preferred_element_type=F32)
        lvl += 1

    eg = jnp.exp(gcol_s[...])
    wu_s[:, 0:dh] = k_s[...] * (beta_s[...] * eg)
    wu_s[:, dh:2 * dh] = v_s[...] * beta_s[...]
    for h in heads:
        wu_s[hrows(h), :] = jnp.dot(_blockdiag(t_s[crows(h), :], bdmask), wu_s[hrows(h), :].astype(BF16),
                                    preferred_element_type=F32)

    q_s[...] = q_s[...] * eg
    k_s[...] = k_s[...] * jnp.exp(glast_s[...] - gcol_s[...])
    for h in heads:
        aw = jnp.dot(_blockdiag(qk_s[crows(h), :], bdmask), wu_s[hrows(h), :].astype(BF16),
                     preferred_element_type=F32)
        o0_s[hrows(h), :] = aw[:, dh:2 * dh]
        q_s[hrows(h), :] = q_s[hrows(h), :] - aw[:, 0:dh]
    for h in heads:
        for c in range(nc):
            i = h * nc + c
            r = rows(h, c)
            mn = _dot_tn(k_s[r, :], wu_s[r, :])
            mq_s[i, 0:dh, :] = (-mn[:, 0:dh]).astype(BF16)
            mq_s[i, dh:dh + CHUNK, :] = q_s[r, :].astype(BF16)
            n_s[i] = mn[:, dh:2 * dh]

    for c in range(nc):
        for h in heads:
            i = h * nc + c
            r = rows(h, c)
            s = st_s[h]
            res = jnp.dot(mq_s[i], s.astype(BF16), preferred_element_type=F32)
            gl = jnp.exp(glast_s[r.start:r.start + 1, :])
            st_s[h] = s * gl + res[0:dh] + n_s[i]
            oo_s[r, :] = res[dh:dh + CHUNK] + o0_s[r, :]

    o = oo_s[...]
    on = o * lax.rsqrt(jnp.mean(o * o, axis=-1, keepdims=True) + EPS) * nw_ref[...]
    for h in heads:
        o_ref[0, :, h * dh:(h + 1) * dh] = (on[hrows(h), :]
                                            * za_ref[0, :, h * dh:(h + 1) * dh].astype(F32)).astype(BF16)


def _delta(q, k, v, za, ba, gpar, norm_w):
    b, seq, _ = za.shape
    lblk = DELTA_LBLK
    blk = pl.BlockSpec((1, lblk, DN_WIDTH), lambda bi, li: (bi, li, 0))
    wide = pltpu.VMEM((DELTA_R, DN_HEAD_DIM), F32)
    cat = pltpu.VMEM((DN_HEADS * CHUNK, DELTA_NC * CHUNK), F32)
    return pl.pallas_call(
        _delta_kernel,
        grid=(b, seq // lblk),
        in_specs=[blk, blk, blk, blk,
                  pl.BlockSpec((1, lblk, LANES), lambda bi, li: (bi, li, 0)),
                  pl.BlockSpec((8, LANES), lambda bi, li: (0, 0)),
                  pl.BlockSpec((1, DN_HEAD_DIM), lambda bi, li: (0, 0))],
        out_specs=blk,
        out_shape=jax.ShapeDtypeStruct((b, seq, DN_WIDTH), BF16),
        scratch_shapes=[wide] * 3
        + [pltpu.VMEM((DN_HEADS, DN_HEAD_DIM, DN_HEAD_DIM), F32)]
        + [wide] * 3
        + [cat] * 4
        + [pltpu.VMEM((DELTA_R, 2 * DN_HEAD_DIM), F32),
           pltpu.VMEM((DELTA_NI, DN_HEAD_DIM + CHUNK, DN_HEAD_DIM), BF16),
           pltpu.VMEM((DELTA_NI, DN_HEAD_DIM, DN_HEAD_DIM), F32),
           wide, wide],
        compiler_params=_cparams(("arbitrary", "arbitrary")),
        name="delta",
    )(q, k, v, za, ba, gpar, norm_w)


def _s5_prep_kernel(bdt_ref, bd_ref, cd_ref, lrow_ref, w_ref, pint_ref, pout_ref, mul_ref, col_s):
    hi = lax.Precision.HIGHEST
    dt = jnp.exp(lrow_ref[0, 2:3, :])
    lr = jnp.minimum(lrow_ref[0, 0:1, :], -1e-4)
    li = lrow_ref[0, 1:2, :]

    def power(n):
        mag = jnp.exp(lr * dt * n)
        return mag * jnp.cos(li * dt * n), mag * jnp.sin(li * dt * n)

    ab_re, ab_im = power(1.0)
    den = lr * lr + li * li
    f_re = ((ab_re - 1.0) * lr + ab_im * li) / den
    f_im = (ab_im * lr - (ab_re - 1.0) * li) / den

    bt_re, bt_im = bdt_ref[0, 0], bdt_ref[0, 1]
    bbt_re = bt_re * f_re - bt_im * f_im
    bbt_im = bt_im * f_re + bt_re * f_im
    c_re, c_im = cd_ref[0, 0], cd_ref[0, 1]
    w_ref[...] = jnp.zeros_like(w_ref)
    col_s[...] = jnp.zeros_like(col_s)
    for tau in range(S5_T + 1):
        p_re, p_im = power(float(tau))
        col_s[tau:tau + 1, :] = p_re
        col_s[16 + tau:17 + tau, :] = p_im
        if tau == S5_T:
            break
        l_re = bbt_re * p_re - bbt_im * p_im
        l_im = bbt_re * p_im + bbt_im * p_re
        tile = (jnp.dot(l_re, c_re, preferred_element_type=F32, precision=hi)
                - jnp.dot(l_im, c_im, preferred_element_type=F32, precision=hi)).astype(BF16)
        for tl in range(S5_T - tau):
            j = tl + tau
            w_ref[0, tl * LANES:(tl + 1) * LANES, j * LANES:(j + 1) * LANES] = tile
    col_s[32:33, :] = f_re
    col_s[33:34, :] = f_im
    nn = lax.broadcasted_iota(jnp.int32, (S5_MULROWS, S5_SP), 0).astype(F32) * float(S5_T)
    m_re, m_im = power(nn)
    mul_ref[0, 0] = m_re
    mul_ref[0, 1] = m_im
    colt = col_s[...].T
    bc = lambda k: jnp.broadcast_to(colt[:, k:k + 1], (S5_SP, LANES))
    cf_re, cf_im = bc(32), bc(33)
    b_re, b_im = bd_ref[0, 0], bd_ref[0, 1]
    bb_re = b_re * cf_re - b_im * cf_im
    bb_im = b_im * cf_re + b_re * cf_im
    for tl in range(S5_T):
        p_re, p_im = bc(S5_T - 1 - tl), bc(16 + S5_T - 1 - tl)
        pint_ref[0, 0:S5_SP, tl * LANES:(tl + 1) * LANES] = (bb_re * p_re - bb_im * p_im).astype(BF16)
        pint_ref[0, S5_SP:2 * S5_SP, tl * LANES:(tl + 1) * LANES] = (bb_re * p_im + bb_im * p_re).astype(BF16)
    for j in range(S5_T):
        p_re, p_im = bc(j + 1), bc(16 + j + 1)
        pout_ref[0, 0:S5_SP, j * LANES:(j + 1) * LANES] = (c_re * p_re - c_im * p_im).astype(BF16)
        pout_ref[0, S5_SP:2 * S5_SP, j * LANES:(j + 1) * LANES] = (-(c_re * p_im + c_im * p_re)).astype(BF16)


def _s5_prep(bdt, bd, cd, lrow):
    per = lambda a: pl.BlockSpec((1,) + a.shape[1:], lambda qi: (qi,) + (0,) * (a.ndim - 1))
    outs = [jax.ShapeDtypeStruct((S5_NQ, S5_TW, S5_TW), BF16),
            jax.ShapeDtypeStruct((S5_NQ, 2 * S5_SP, S5_TW), BF16),
            jax.ShapeDtypeStruct((S5_NQ, 2 * S5_SP, S5_TW), BF16),
            jax.ShapeDtypeStruct((S5_NQ, 2, S5_MULROWS, S5_SP), F32)]
    return pl.pallas_call(
        _s5_prep_kernel,
        grid=(S5_NQ,),
        in_specs=[per(bdt), per(bd), per(cd), per(lrow)],
        out_specs=[per(o) for o in outs],
        out_shape=outs,
        scratch_shapes=[pltpu.VMEM((LANES, S5_SP), F32)],
        compiler_params=_cparams(("arbitrary",)),
        name="s5prep",
    )(bdt, bd, cd, lrow)


def _s5_kernel(u_ref, w_ref, pint_ref, pout_ref, mul_ref, dsk_ref, y_ref, sre_s, sim_s, carry_s, seg_s, y_s):
    @pl.when(pl.program_id(2) == 0)
    def _reset():
        carry_s[...] = jnp.zeros_like(carry_s)

    xs = [[u_ref[0, 0, pl.ds(i * S5_T + tl, S5_SUB, stride=S5_SEG), :] for i in range(S5_NVR)]
          for tl in range(S5_T)]
    uc = jnp.concatenate([jnp.concatenate(xs[tl], axis=0).astype(BF16) for tl in range(S5_T)], axis=1)
    sin = lax.dot_general(uc, pint_ref[0], (((1,), (1,)), ((), ())), preferred_element_type=F32)
    sre_s[...] = sin[:, 0:S5_SP]
    sim_s[...] = sin[:, S5_SP:2 * S5_SP]

    l_re = mul_ref[0, 0, 1:2, :]
    l_im = mul_ref[0, 1, 1:2, :]

    y_s[...] = jnp.dot(uc, w_ref[0], preferred_element_type=F32)
    pr = jnp.zeros((S5_SUB, S5_SP), F32)
    pi = jnp.zeros((S5_SUB, S5_SP), F32)
    for i in range(S5_NVR):
        rs = slice(i * S5_SUB, (i + 1) * S5_SUB)
        pr, pi = sre_s[rs, :] + l_re * pr - l_im * pi, sim_s[rs, :] + l_re * pi + l_im * pr
        sre_s[rs, :] = pr
        sim_s[rs, :] = pi
    e_re, e_im = pr, pi

    g_re = mul_ref[0, 0, S5_NVR:S5_NVR + 1, :]
    g_im = mul_ref[0, 1, S5_NVR:S5_NVR + 1, :]
    seg_s[0] = e_re
    seg_s[1] = e_im
    c_re = carry_s[0:1, :]
    c_im = carry_s[1:2, :]
    for sgi in range(S5_SUB):
        seg_s[2, sgi:sgi + 1, :] = c_re
        seg_s[3, sgi:sgi + 1, :] = c_im
        n_re = g_re * c_re - g_im * c_im + seg_s[0, sgi:sgi + 1, :]
        n_im = g_re * c_im + g_im * c_re + seg_s[1, sgi:sgi + 1, :]
        c_re, c_im = n_re, n_im
    carry_s[0:1, :] = c_re
    carry_s[1:2, :] = c_im

    cs_re = jnp.broadcast_to(seg_s[2][None], (S5_NVR, S5_SUB, S5_SP)).reshape(S5_NCH, S5_SP)
    cs_im = jnp.broadcast_to(seg_s[3][None], (S5_NVR, S5_SUB, S5_SP)).reshape(S5_NCH, S5_SP)
    rep = lambda a: jnp.broadcast_to(a[:, None, :], (S5_NVR, S5_SUB, S5_SP)).reshape(S5_NCH, S5_SP)
    m_re = rep(mul_ref[0, 0, 0:S5_NVR, :])
    m_im = rep(mul_ref[0, 1, 0:S5_NVR, :])
    zrow = jnp.zeros((S5_SUB, S5_SP), F32)
    sh_re = jnp.concatenate([zrow, sre_s[0:S5_NCH - S5_SUB, :]], axis=0)
    sh_im = jnp.concatenate([zrow, sim_s[0:S5_NCH - S5_SUB, :]], axis=0)
    pv_re = sh_re + m_re * cs_re - m_im * cs_im
    pv_im = sh_im + m_re * cs_im + m_im * cs_re
    xprev = jnp.concatenate([pv_re.astype(BF16), pv_im.astype(BF16)], axis=1)
    y = y_s[...] + jnp.dot(xprev, pout_ref[0], preferred_element_type=F32)
    dq = dsk_ref[0]
    for j in range(S5_T):
        for i in range(S5_NVR):
            y_ref[0, 0, pl.ds(i * S5_T + j, S5_SUB, stride=S5_SEG), :] = (
                y[i * S5_SUB:(i + 1) * S5_SUB, j * LANES:(j + 1) * LANES] + xs[j][i] * dq)


def _s5_params(lam_re, lam_im, log_dt, b_re, b_im, c_re, c_im, d_skip):
    same = jnp.eye(S5_GPT, dtype=bool)[None, :, None, :, None]

    def bdiag(a):
        a = a.reshape(S5_NQ, S5_GPT, S5_STATE, 1, S5_GROUP)
        return jnp.where(same, a, 0.0).reshape(S5_NQ, S5_SP, LANES)

    bd = jnp.stack([bdiag(b_re), bdiag(b_im)], axis=1)
    bdt = jnp.swapaxes(bd, 2, 3)
    cd = jnp.stack([bdiag(jnp.swapaxes(c_re, 1, 2)), bdiag(jnp.swapaxes(c_im, 1, 2))], axis=1)
    dtb = jnp.broadcast_to(log_dt[:, None], lam_re.shape)
    lrow = jnp.stack([lam_re.reshape(S5_NQ, S5_SP), lam_im.reshape(S5_NQ, S5_SP), dtb.reshape(S5_NQ, S5_SP)], axis=1)
    lrow = jnp.pad(lrow, ((0, 0), (0, 5), (0, 0)))
    dsk = d_skip.reshape(S5_NQ, 1, LANES)
    return bdt, bd, cd, lrow, dsk


def _s5(u4, bdt, bd, cd, lrow, dsk):
    _, b, seq, _ = u4.shape
    w, pint, pout, mul = _s5_prep(bdt, bd, cd, lrow)
    perq = lambda a: pl.BlockSpec((1,) + a.shape[1:], lambda qi, bi, li: (qi,) + (0,) * (a.ndim - 1))
    blk = pl.BlockSpec((1, 1, S5_LB, LANES), lambda qi, bi, li: (qi, bi, li, 0))
    return pl.pallas_call(
        _s5_kernel,
        grid=(S5_NQ, b, seq // S5_LB),
        in_specs=[blk, perq(w), perq(pint), perq(pout), perq(mul), perq(dsk)],
        out_specs=blk,
        out_shape=jax.ShapeDtypeStruct((S5_NQ, b, seq, LANES), F32),
        scratch_shapes=[pltpu.VMEM((S5_NCH, S5_SP), F32),
                        pltpu.VMEM((S5_NCH, S5_SP), F32),
                        pltpu.VMEM((8, S5_SP), F32),
                        pltpu.VMEM((4, S5_SUB, S5_SP), F32),
                        pltpu.VMEM((S5_NCH, S5_TW), F32)],
        compiler_params=_cparams(("arbitrary", "arbitrary", "arbitrary")),
        name="s5",
    )(u4, w, pint, pout, mul, dsk)


FINAL_TM = 512
FINAL_SUB = 256


def _final_kernel(x_ref, og_ref, ys_ref, zb_ref, ga_ref, gb_ref, mod_ref,
                  wpa_ref, glw_ref, glb_ref, wpb_ref, wo_ref, fnw_ref, out_ref):
    gate = mod_ref[0, :, 2 * D_MODEL:3 * D_MODEL]
    for part in range(FINAL_TM // FINAL_SUB):
        rs = slice(part * FINAL_SUB, (part + 1) * FINAL_SUB)
        ya = jnp.dot(og_ref[rs, :], wpa_ref[...], preferred_element_type=F32)
        ys = jnp.concatenate([ys_ref[qt, rs, :] for qt in range(S5_NQ)], axis=1)
        gy = 0.5 * ys * (1.0 + jnp.tanh(math.sqrt(2.0 / math.pi) * (ys + 0.044715 * (ys * ys * ys))))
        glu = jnp.dot(gy.astype(BF16), glw_ref[...], preferred_element_type=F32) + glb_ref[...]
        y2 = gy * _sigmoid(glu)
        yb = jnp.dot((y2 * zb_ref[rs, :].astype(F32)).astype(BF16), wpb_ref[...], preferred_element_type=F32)
        merged = ga_ref[rs, :].astype(F32) * ya + gb_ref[rs, :].astype(F32) * yb
        mo = jnp.dot(merged.astype(BF16), wo_ref[...], preferred_element_type=F32)
        xo = x_ref[rs, :] + gate * mo
        out_ref[rs, :] = xo * lax.rsqrt(jnp.mean(xo * xo, axis=-1, keepdims=True) + EPS) * fnw_ref[...]


def _final(x2, og, ys, zb, ga, gb, mod3, wpa, glw, glb, wpb, wo, fnw, seq):
    n = x2.shape[0]
    tm = FINAL_TM
    per_b = seq // tm
    row = lambda w: pl.BlockSpec((tm, w), lambda i: (i, 0))
    const = lambda a: pl.BlockSpec(a.shape, lambda i: (0,) * a.ndim)
    return pl.pallas_call(
        _final_kernel,
        grid=(n // tm,),
        in_specs=[row(D_MODEL), row(DN_WIDTH), pl.BlockSpec((S5_NQ, tm, LANES), lambda i: (0, i, 0)),
                  row(S5_WIDTH), row(D_MODEL), row(D_MODEL),
                  pl.BlockSpec((1, 1, 3 * D_MODEL), lambda i: (i // per_b, 0, 0)),
                  const(wpa), const(glw), const(glb), const(wpb), const(wo), const(fnw)],
        out_specs=row(D_MODEL),
        out_shape=jax.ShapeDtypeStruct((n, D_MODEL), F32),
        compiler_params=_cparams(("arbitrary",)),
        name="final",
    )(x2, og, ys, zb, ga, gb, mod3, wpa, glw, glb, wpb, wo, fnw)


def kernel(x, c, w_ada, b_ada, w_in, dn_conv_w, dn_a_log, dn_dt_bias, dn_norm_w, w_proj_a,
           s5_lambda_re, s5_lambda_im, s5_log_dt, s5_b_re, s5_b_im, s5_c_re, s5_c_im, s5_d,
           s5_glu_w, s5_glu_b, w_proj_b, w_out, final_norm_w):
    bsz, seq, d = x.shape
    n = bsz * seq
    assert w_ada.shape[0] == 1, "the final rmsnorm is fused into the single layer's epilogue"
    layer = 0
    x2 = x.reshape(n, d)
    c8 = jnp.zeros((8, d), F32).at[:bsz].set(c)
    mod = _ada(c8, w_ada[layer], b_ada[layer][None, :])[:bsz]
    mod3 = mod.reshape(bsz, 1, 3 * d)

    wl = w_in[layer]
    o_ba = 4 * DN_WIDTH
    o_rest = o_ba + 2 * DN_HEADS
    w = jnp.concatenate([wl[:, :o_ba], wl[:, o_rest:], wl[:, o_ba:o_rest],
                         jnp.zeros((d, LANES - 2 * DN_HEADS), wl.dtype)], axis=1).astype(BF16)
    q, k, v, za, ba, u, zb, ga, gb = _inproj(x2, mod3, w, dn_conv_w[layer], seq)

    gpar = jnp.zeros((8, LANES), F32)
    gpar = gpar.at[0, DN_HEADS:2 * DN_HEADS].set(dn_a_log[layer])
    gpar = gpar.at[1, DN_HEADS:2 * DN_HEADS].set(dn_dt_bias[layer])
    r3 = lambda a: a.reshape(bsz, seq, a.shape[-1])
    og = _delta(r3(q), r3(k), r3(v), r3(za), r3(ba), gpar, dn_norm_w[layer][None, :])

    ys = _s5(u.reshape(S5_NQ, bsz, seq, LANES),
             *_s5_params(s5_lambda_re[layer], s5_lambda_im[layer], s5_log_dt[layer], s5_b_re[layer],
                         s5_b_im[layer], s5_c_re[layer], s5_c_im[layer], s5_d[layer])).reshape(S5_NQ, n, LANES)

    out = _final(x2, og.reshape(n, DN_WIDTH), ys, zb, ga, gb, mod3,
                 w_proj_a[layer].astype(BF16), s5_glu_w[layer].astype(BF16), s5_glu_b[layer][None, :],
                 w_proj_b[layer].astype(BF16), w_out[layer].astype(BF16), final_norm_w[None, :], seq)
    return out.reshape(bsz, seq, d)
```
